```python
import jax, jax.numpy as jnp
from jax import lax
import numpy as np

D_MODEL = 1024
BATCH = 2
SEQ = 16384
DEPTH = 2
DEC_BATCH = 16
DEC_SEQ = 4096
PAST_LEN = 128

N_MEM = 256
EPS = 1e-6
MLA_HEADS = 8
QK_NOPE = 64
QK_ROPE = 32
V_HEAD = 64
Q_RANK = 384
KV_RANK = 256
ROPE_THETA = 10000.0
Q_BLOCK = 128
FNET_GROUPS = 4
FNET_CH = 128
IN_W = Q_RANK + KV_RANK + QK_ROPE + FNET_GROUPS * FNET_CH
MIX_W = MLA_HEADS * V_HEAD + FNET_GROUPS * FNET_CH
CONV_K = 31
XA_HEADS = 4
XA_HEAD_DIM = D_MODEL // XA_HEADS
D_FF = -(-(8 * D_MODEL) // (3 * 256)) * 256
N_EVEN = (DEPTH + 1) // 2
N_ODD = DEPTH // 2

kernel_name = "hybrid_mla_fnet_conformer_encoder"


def _rmsnorm(x, g):
    xf = x.astype(jnp.float32)
    y = xf * lax.rsqrt(jnp.mean(xf * xf, axis=-1, keepdims=True) + EPS)
    return (y * g.astype(jnp.float32)).astype(x.dtype)


def _layernorm(x, g, b):
    xf = x.astype(jnp.float32)
    mu = jnp.mean(xf, axis=-1, keepdims=True)
    xc = xf - mu
    y = xc * lax.rsqrt(jnp.mean(xc * xc, axis=-1, keepdims=True) + EPS)
    return (y * g.astype(jnp.float32) + b.astype(jnp.float32)).astype(x.dtype)


def _rope(x):
    s = x.shape[1]
    half = QK_ROPE // 2
    inv = ROPE_THETA ** (-jnp.arange(half, dtype=jnp.float32) / half)
    ang = jnp.arange(s, dtype=jnp.float32)[:, None] * inv[None, :]
    shape = (1, s) + (1,) * (x.ndim - 3) + (half,)
    cos = jnp.cos(ang).reshape(shape)
    sin = jnp.sin(ang).reshape(shape)
    xf = x.astype(jnp.float32)
    x1, x2 = xf[..., :half], xf[..., half:]
    out = jnp.concatenate([x1 * cos - x2 * sin, x1 * sin + x2 * cos], axis=-1)
    return out.astype(x.dtype)


def _mla_attention(q_nope, q_rope, k_nope, k_rope, v):
    b, s, h, _ = q_nope.shape
    nb = s // Q_BLOCK
    scale = (QK_NOPE + QK_ROPE) ** -0.5
    qn = q_nope.reshape(b, nb, Q_BLOCK, h, QK_NOPE).transpose(1, 0, 2, 3, 4)
    qr = q_rope.reshape(b, nb, Q_BLOCK, h, QK_ROPE).transpose(1, 0, 2, 3, 4)

    def one_block(args):
        qn_blk, qr_blk = args
        sc = (jnp.einsum('bqhd,bkhd->bhqk', qn_blk, k_nope, preferred_element_type=jnp.float32)
              + jnp.einsum('bqhr,bkr->bhqk', qr_blk, k_rope, preferred_element_type=jnp.float32))
        p = jax.nn.softmax(sc * scale, axis=-1).astype(v.dtype)
        return jnp.einsum('bhqk,bkhd->bqhd', p, v)

    out = lax.map(one_block, (qn, qr))
    return out.transpose(1, 0, 2, 3, 4).reshape(b, s, h * V_HEAD)


def _mla_fnet_mixer(h, w_in, g_q, w_uq, g_kv, w_ukv, w_out):
    b, s, _ = h.shape
    z = h @ w_in
    c_q, c_kv, k_r, f = jnp.split(
        z, [Q_RANK, Q_RANK + KV_RANK, Q_RANK + KV_RANK + QK_ROPE], axis=-1)
    q = (_rmsnorm(c_q, g_q) @ w_uq).reshape(b, s, MLA_HEADS, QK_NOPE + QK_ROPE)
    q_nope, q_rope = q[..., :QK_NOPE], _rope(q[..., QK_NOPE:])
    kv = (_rmsnorm(c_kv, g_kv) @ w_ukv).reshape(b, s, MLA_HEADS, QK_NOPE + V_HEAD)
    k_nope, v = kv[..., :QK_NOPE], kv[..., QK_NOPE:]
    k_rope = _rope(k_r)
    attn = _mla_attention(q_nope, q_rope, k_nope, k_rope, v)
    fg = f.reshape(b, s, FNET_GROUPS, FNET_CH).astype(jnp.float32)
    fo = jnp.real(jnp.fft.fft2(fg, axes=(1, 3), norm='ortho'))
    fo = fo.astype(h.dtype).reshape(b, s, FNET_GROUPS * FNET_CH)
    return jnp.concatenate([attn, fo], axis=-1) @ w_out


def _conformer_conv(h, w_pw1, b_pw1, w_dw, b_dw, g_ln, b_ln, w_pw2, b_pw2):
    u = h @ w_pw1 + b_pw1
    a, gate = jnp.split(u, 2, axis=-1)
    u = a * jax.nn.sigmoid(gate)
    u = lax.conv_general_dilated(
        u, w_dw[:, None, :].astype(u.dtype), window_strides=(1,),
        padding=[(CONV_K // 2, CONV_K // 2)],
        dimension_numbers=('NWC', 'WIO', 'NWC'),
        feature_group_count=D_MODEL) + b_dw
    u = jax.nn.silu(_layernorm(u, g_ln, b_ln))
    return u @ w_pw2 + b_pw2


def _mem_xattn(h, m, wq, wk, wv, wo):
    b, s, _ = h.shape
    nm = m.shape[1]
    q = (h @ wq).reshape(b, s, XA_HEADS, XA_HEAD_DIM)
    k = (m @ wk).reshape(b, nm, XA_HEADS, XA_HEAD_DIM)
    v = (m @ wv).reshape(b, nm, XA_HEADS, XA_HEAD_DIM)
    sc = jnp.einsum('bqhd,bkhd->bhqk', q, k, preferred_element_type=jnp.float32)
    p = jax.nn.softmax(sc * XA_HEAD_DIM ** -0.5, axis=-1).astype(v.dtype)
    o = jnp.einsum('bhqk,bkhd->bqhd', p, v).reshape(b, s, D_MODEL)
    return o @ wo


def _swiglu(h, w1, w3, w2):
    return (jax.nn.silu(h @ w1) * (h @ w3)) @ w2


def _trunk(x, mem, p):
    for i in range(DEPTH):
        j = i // 2
        h = _rmsnorm(x, p['g_mix'][i])
        if i % 2 == 0:
            x = x + _mla_fnet_mixer(h, p['ab_w_in'][j], p['mla_g_q'][j], p['mla_w_uq'][j],
                                    p['mla_g_kv'][j], p['mla_w_ukv'][j], p['ab_w_out'][j])
        else:
            x = x + _conformer_conv(h, p['conv_w_pw1'][j], p['conv_b_pw1'][j],
                                    p['conv_w_dw'][j], p['conv_b_dw'][j],
                                    p['conv_g_ln'][j], p['conv_b_ln'][j],
                                    p['conv_w_pw2'][j], p['conv_b_pw2'][j])
        x = x + _mem_xattn(_rmsnorm(x, p['g_xq'][i]), _rmsnorm(mem, p['g_xkv'][i]),
                           p['xa_wq'][i], p['xa_wk'][i], p['xa_wv'][i], p['xa_wo'][i])
        x = x + _swiglu(_rmsnorm(x, p['g_ffn'][i]), p['ffn_w1'][i], p['ffn_w3'][i], p['ffn_w2'][i])
    return _rmsnorm(x, p['g_final'])


def setup_inputs(seed: int = 0) -> dict:
    key = jax.random.key(seed)
    ks = iter(jax.random.split(key, 48))

    def w(shape, fan_in):
        return jax.random.normal(next(ks), shape, jnp.float32) * (fan_in ** -0.5)

    def gain(shape):
        return 1.0 + 0.02 * jax.random.normal(next(ks), shape, jnp.float32)

    def bias(shape):
        return 0.02 * jax.random.normal(next(ks), shape, jnp.float32)

    return {
        'x_prompt': jax.random.normal(next(ks), (BATCH, SEQ, D_MODEL), jnp.float32),
        'x_sample': jax.random.normal(next(ks), (DEC_BATCH, DEC_SEQ, D_MODEL), jnp.float32),
        'mem_prompt': jax.random.normal(next(ks), (BATCH, N_MEM, D_MODEL), jnp.float32),
        'mem_sample': jax.random.normal(next(ks), (DEC_BATCH, N_MEM, D_MODEL), jnp.float32),
        'g_mix': gain((DEPTH, D_MODEL)),
        'g_xq': gain((DEPTH, D_MODEL)),
        'g_xkv': gain((DEPTH, D_MODEL)),
        'xa_wq': w((DEPTH, D_MODEL, D_MODEL), D_MODEL),
        'xa_wk': w((DEPTH, D_MODEL, D_MODEL), D_MODEL),
        'xa_wv': w((DEPTH, D_MODEL, D_MODEL), D_MODEL),
        'xa_wo': w((DEPTH, D_MODEL, D_MODEL), D_MODEL),
        'g_ffn': gain((DEPTH, D_MODEL)),
        'ffn_w1': w((DEPTH, D_MODEL, D_FF), D_MODEL),
        'ffn_w3': w((DEPTH, D_MODEL, D_FF), D_MODEL),
        'ffn_w2': w((DEPTH, D_FF, D_MODEL), D_FF),
        'ab_w_in': w((N_EVEN, D_MODEL, IN_W), D_MODEL),
        'mla_g_q': gain((N_EVEN, Q_RANK)),
        'mla_w_uq': w((N_EVEN, Q_RANK, MLA_HEADS * (QK_NOPE + QK_ROPE)), Q_RANK),
        'mla_g_kv': gain((N_EVEN, KV_RANK)),
        'mla_w_ukv': w((N_EVEN, KV_RANK, MLA_HEADS * (QK_NOPE + V_HEAD)), KV_RANK),
        'ab_w_out': w((N_EVEN, MIX_W, D_MODEL), MIX_W),
        'conv_w_pw1': w((N_ODD, D_MODEL, 2 * D_MODEL), D_MODEL),
        'conv_b_pw1': bias((N_ODD, 2 * D_MODEL)),
        'conv_w_dw': w((N_ODD, CONV_K, D_MODEL), CONV_K),
        'conv_b_dw': bias((N_ODD, D_MODEL)),
        'conv_g_ln': gain((N_ODD, D_MODEL)),
        'conv_b_ln': bias((N_ODD, D_MODEL)),
        'conv_w_pw2': w((N_ODD, D_MODEL, D_MODEL), D_MODEL),
        'conv_b_pw2': bias((N_ODD, D_MODEL)),
        'g_final': gain((D_MODEL,)),
    }


def reference(x_prompt, x_sample, mem_prompt, mem_sample,
              g_mix, g_xq, g_xkv, xa_wq, xa_wk, xa_wv, xa_wo,
              g_ffn, ffn_w1, ffn_w3, ffn_w2,
              ab_w_in, mla_g_q, mla_w_uq, mla_g_kv, mla_w_ukv, ab_w_out,
              conv_w_pw1, conv_b_pw1, conv_w_dw, conv_b_dw, conv_g_ln, conv_b_ln,
              conv_w_pw2, conv_b_pw2, g_final):
    p = {
        'g_mix': g_mix, 'g_xq': g_xq, 'g_xkv': g_xkv,
        'xa_wq': xa_wq, 'xa_wk': xa_wk, 'xa_wv': xa_wv, 'xa_wo': xa_wo,
        'g_ffn': g_ffn, 'ffn_w1': ffn_w1, 'ffn_w3': ffn_w3, 'ffn_w2': ffn_w2,
        'ab_w_in': ab_w_in, 'mla_g_q': mla_g_q, 'mla_w_uq': mla_w_uq,
        'mla_g_kv': mla_g_kv, 'mla_w_ukv': mla_w_ukv, 'ab_w_out': ab_w_out,
        'conv_w_pw1': conv_w_pw1, 'conv_b_pw1': conv_b_pw1,
        'conv_w_dw': conv_w_dw, 'conv_b_dw': conv_b_dw,
        'conv_g_ln': conv_g_ln, 'conv_b_ln': conv_b_ln,
        'conv_w_pw2': conv_w_pw2, 'conv_b_pw2': conv_b_pw2,
        'g_final': g_final,
    }
    y_prompt = _trunk(x_prompt, mem_prompt, p)
    y_sample = _trunk(x_sample, mem_sample, p)
    return (y_prompt, y_sample)
```

```python
import functools
import math

import jax
import jax.numpy as jnp
from jax import lax
from jax.experimental import pallas as pl
from jax.experimental.pallas import tpu as pltpu

MLA_HEADS = 8
QK_NOPE = 64
QK_ROPE = 32
V_HEAD = 64
ROPE_THETA = 10000.0
FNET_GROUPS = 4
FNET_CH = 128
XA_HEADS = 4
EPS = 1e-6

LANES = 128
HEAD_PAD = 128
VMEM_LIMIT = 52 * 1024 * 1024

F32 = jnp.float32
BF16 = jnp.bfloat16


def _params(*sem):
    return pltpu.CompilerParams(dimension_semantics=sem, vmem_limit_bytes=VMEM_LIMIT)


def _rms(x, g):
    ms = jnp.mean(x * x, axis=-1, keepdims=True)
    return x * lax.rsqrt(ms + EPS) * g


def _dot(a, b):
    return jnp.dot(a, b, preferred_element_type=F32)


def _dot_exact(a, b):
    return jnp.dot(a, b, preferred_element_type=F32, precision=lax.Precision.HIGHEST)


def _const_spec(shape):
    nd = len(shape)
    return pl.BlockSpec(shape, lambda *_: (0,) * nd)


def _tile(n, want):
    t = min(n, want)
    assert n % t == 0, (n, t)
    return t


def _mixer_in_kernel(x_ref, g_ref, win_ref, gq_ref, wuq_ref, wuqr_ref, gkv_ref, wuk_ref, wuv_ref,
                     cos_ref, sin_ref, q_ref, k_ref, v_ref, f_ref, *, q_rank, kv_rank, f_w, q_scale):
    x = x_ref[0]
    h = _rms(x, g_ref[...]).astype(BF16)
    z = _dot(h, win_ref[...])
    o = 0
    cq = z[:, o:o + q_rank]; o += q_rank
    ckv = z[:, o:o + kv_rank]; o += kv_rank
    f_ref[0] = z[:, o:o + f_w]; o += f_w
    kr = z[:, o:o + HEAD_PAD]; o += HEAD_PAD
    krr = z[:, o:o + HEAD_PAD]
    cos = cos_ref[...]
    sin = sin_ref[...]
    cqn = _rms(cq, gq_ref[...]).astype(BF16)
    q = _dot(cqn, wuq_ref[...])
    qr = _dot(cqn, wuqr_ref[...])
    ckvn = _rms(ckv, gkv_ref[...]).astype(BF16)
    k = _dot(ckvn, wuk_ref[...])
    v_ref[0] = _dot(ckvn, wuv_ref[...]).astype(BF16)
    k_rope = kr * cos + krr * sin
    for hd in range(MLA_HEADS):
        sl = slice(hd * HEAD_PAD, (hd + 1) * HEAD_PAD)
        q_ref[0, :, sl] = ((q[:, sl] * cos + qr[:, sl] * sin) * q_scale).astype(BF16)
        k_ref[0, :, sl] = (k[:, sl] + k_rope).astype(BF16)


def _mixer_in(x, g, win, gq, wuq, wuqr, gkv, wuk, wuv, cos, sin, *, q_rank, kv_rank, f_w, q_scale):
    b, s, d = x.shape
    tm = _tile(s, 512)
    hw = MLA_HEADS * HEAD_PAD
    vw = MLA_HEADS * V_HEAD
    tok = lambda w: pl.BlockSpec((1, tm, w), lambda bi, i: (bi, i, 0))
    rope = pl.BlockSpec((tm, HEAD_PAD), lambda bi, i: (i, 0))
    kern = functools.partial(_mixer_in_kernel, q_rank=q_rank, kv_rank=kv_rank, f_w=f_w, q_scale=q_scale)
    return pl.pallas_call(
        kern,
        grid=(b, s // tm),
        in_specs=[tok(d), _const_spec(g.shape), _const_spec(win.shape), _const_spec(gq.shape),
                  _const_spec(wuq.shape), _const_spec(wuqr.shape), _const_spec(gkv.shape),
                  _const_spec(wuk.shape), _const_spec(wuv.shape), rope, rope],
        out_specs=[tok(hw), tok(hw), tok(vw), tok(f_w)],
        out_shape=[jax.ShapeDtypeStruct((b, s, hw), BF16), jax.ShapeDtypeStruct((b, s, hw), BF16),
                   jax.ShapeDtypeStruct((b, s, vw), BF16), jax.ShapeDtypeStruct((b, s, f_w), F32)],
        compiler_params=_params("parallel", "parallel"),
        name="mixer_in",
    )(x, g, win, gq, wuq, wuqr, gkv, wuk, wuv, cos, sin)


def _flash_kernel(q_ref, k_ref, v_ref, o_ref, *, tk):
    tq = q_ref.shape[1]
    nk = k_ref.shape[1] // tk
    outs = []
    for a in range(2):
        hs = slice(a * HEAD_PAD, (a + 1) * HEAD_PAD)
        q = q_ref[0, :, hs]

        def body(j, carry, hs=hs, q=q):
            m, l, acc = carry
            ks = pl.multiple_of(j * tk, tk)
            k = k_ref[0, pl.ds(ks, tk), hs]
            v = v_ref[0, pl.ds(ks, tk), :]
            s = lax.dot_general(q, k, (((1,), (1,)), ((), ())), preferred_element_type=F32)
            m_new = jnp.maximum(m, jnp.max(s, axis=1, keepdims=True))
            alpha = jnp.exp(m - m_new)
            p = jnp.exp(s - m_new)
            l = alpha * l + jnp.sum(p, axis=1, keepdims=True)
            acc = alpha * acc + _dot(p.astype(BF16), v)
            return m_new, l, acc

        init = (jnp.full((tq, 1), -jnp.inf, F32), jnp.zeros((tq, 1), F32),
                jnp.zeros((tq, 2 * V_HEAD), F32))
        _, l, acc = lax.fori_loop(0, nk, body, init)
        outs.append(acc / l)
    lane = lax.broadcasted_iota(jnp.int32, (tq, 2 * V_HEAD), 1)
    o_ref[0] = jnp.where(lane < V_HEAD, outs[0], outs[1]).astype(BF16)


def _flash(q, k, v):
    b, s, _ = q.shape
    tq = _tile(s, 256)
    tk = _tile(s, 512)
    pairs = MLA_HEADS // 2
    return pl.pallas_call(
        functools.partial(_flash_kernel, tk=tk),
        grid=(b, pairs, s // tq),
        in_specs=[pl.BlockSpec((1, tq, 2 * HEAD_PAD), lambda bi, hp, i: (bi, i, hp)),
                  pl.BlockSpec((1, s, 2 * HEAD_PAD), lambda bi, hp, i: (bi, 0, hp)),
                  pl.BlockSpec((1, s, 2 * V_HEAD), lambda bi, hp, i: (bi, 0, hp))],
        out_specs=pl.BlockSpec((1, tq, 2 * V_HEAD), lambda bi, hp, i: (bi, i, hp)),
        out_shape=jax.ShapeDtypeStruct((b, s, MLA_HEADS * V_HEAD), BF16),
        compiler_params=_params("parallel", "parallel", "arbitrary"),
        name="mla_flash",
    )(q, k, v)


def _dft_a_kernel(x_ref, c_ref, s_ref, ar_ref, ai_ref):
    x = x_ref[0]
    ar_ref[0] = _dot_exact(c_ref[...], x)
    ai_ref[0] = -_dot_exact(s_ref[...], x)


def _dft_a(x2, c1, s1):
    b, n1, w = x2.shape
    tc = _tile(w, 4096)
    blk = pl.BlockSpec((1, n1, tc), lambda bi, j: (bi, 0, j))
    sds = jax.ShapeDtypeStruct((b, n1, w), F32)
    return pl.pallas_call(
        _dft_a_kernel,
        grid=(b, w // tc),
        in_specs=[blk, _const_spec(c1.shape), _const_spec(s1.shape)],
        out_specs=[blk, blk],
        out_shape=[sds, sds],
        compiler_params=_params("parallel", "parallel"),
        name="fnet_dft_a",
    )(x2, c1, s1)


def _dft_b_kernel(ar_ref, ai_ref, gr_ref, gi_ref, cc_ref, sc_ref, o_ref, *, kb, scale):
    cc = cc_ref[...]
    sc = sc_ref[...]
    for j in range(kb):
        ar = ar_ref[0, j]
        ai = ai_ref[0, j]
        gr = gr_ref[j]
        gi = gi_ref[j]
        yr = _dot_exact(gr, ar) - _dot_exact(gi, ai)
        yi = _dot_exact(gr, ai) + _dot_exact(gi, ar)
        for grp in range(FNET_GROUPS):
            sl = slice(grp * FNET_CH, (grp + 1) * FNET_CH)
            res = _dot_exact(yr[:, sl], cc) + _dot_exact(yi[:, sl], sc)
            o_ref[0, :, j, sl] = res * scale


def _dft_b(ar, ai, gr, gi, cc, sc, scale):
    b, n1, n2, w = ar.shape
    kb = 8
    assert n1 % kb == 0
    a_blk = pl.BlockSpec((1, kb, n2, w), lambda bi, i: (bi, i, 0, 0))
    g_blk = pl.BlockSpec((kb, n2, n2), lambda bi, i: (i, 0, 0))
    return pl.pallas_call(
        functools.partial(_dft_b_kernel, kb=kb, scale=scale),
        grid=(b, n1 // kb),
        in_specs=[a_blk, a_blk, g_blk, g_blk, _const_spec(cc.shape), _const_spec(sc.shape)],
        out_specs=pl.BlockSpec((1, n2, kb, w), lambda bi, i: (bi, 0, i, 0)),
        out_shape=jax.ShapeDtypeStruct((b, n2, n1, w), F32),
        compiler_params=_params("parallel", "parallel"),
        name="fnet_dft_b",
    )(ar, ai, gr, gi, cc, sc)


def _dft_tables(s):
    n1 = 1 << ((s.bit_length() - 1 + 1) // 2)
    n2 = s // n1
    assert n1 * n2 == s and n1 % 8 == 0, (s, n1, n2)

    def cs(idx, period):
        ang = (2.0 * math.pi / period) * (idx % period).astype(F32)
        return jnp.cos(ang), jnp.sin(ang)

    i1 = jnp.arange(n1, dtype=jnp.int32)
    i2 = jnp.arange(n2, dtype=jnp.int32)
    c1, s1 = cs(i1[:, None] * i1[None, :], n1)
    idx = (i2[None, None, :] * (i1[:, None, None] + n1 * i2[None, :, None])) % s
    gr, gi_pos = cs(idx, s)
    ic = jnp.arange(FNET_CH, dtype=jnp.int32)
    cc, sc = cs(ic[:, None] * ic[None, :], FNET_CH)
    return n1, n2, c1, s1, gr, -gi_pos, cc, sc


def _fnet(f):
    b, s, w = f.shape
    n1, n2, c1, s1, gr, gi, cc, sc = _dft_tables(s)
    ar, ai = _dft_a(f.reshape(b, n1, n2 * w), c1, s1)
    out = _dft_b(ar.reshape(b, n1, n2, w), ai.reshape(b, n1, n2, w), gr, gi, cc, sc,
                 1.0 / math.sqrt(s * FNET_CH))
    return out.reshape(b, s, w)


def _mixer_out_kernel(x_ref, a_ref, f_ref, wa_ref, wf_ref, o_ref):
    o_ref[0] = (x_ref[0] + _dot(a_ref[0], wa_ref[...])
                + _dot(f_ref[0].astype(BF16), wf_ref[...]))


def _mixer_out(x, attn, fo, wa, wf):
    b, s, d = x.shape
    tm = _tile(s, 512)
    tok = lambda w: pl.BlockSpec((1, tm, w), lambda bi, i: (bi, i, 0))
    return pl.pallas_call(
        _mixer_out_kernel,
        grid=(b, s // tm),
        in_specs=[tok(d), tok(attn.shape[-1]), tok(fo.shape[-1]), _const_spec(wa.shape), _const_spec(wf.shape)],
        out_specs=tok(d),
        out_shape=jax.ShapeDtypeStruct(x.shape, F32),
        compiler_params=_params("parallel", "parallel"),
        name="mixer_out",
    )(x, attn, fo, wa, wf)


def _conv_in_kernel(x_ref, g_ref, w_ref, b_ref, u_ref):
    d = x_ref.shape[-1]
    h = _rms(x_ref[0], g_ref[...]).astype(BF16)
    z = _dot(h, w_ref[...]) + b_ref[...]
    u_ref[0] = z[:, :d] * jax.nn.sigmoid(z[:, d:])


def _conv_in(x, g, w, bias):
    b, s, d = x.shape
    tm = _tile(s, 512)
    tok = pl.BlockSpec((1, tm, d), lambda bi, i: (bi, i, 0))
    return pl.pallas_call(
        _conv_in_kernel,
        grid=(b, s // tm),
        in_specs=[tok, _const_spec(g.shape), _const_spec(w.shape), _const_spec(bias.shape)],
        out_specs=tok,
        out_shape=jax.ShapeDtypeStruct(x.shape, F32),
        compiler_params=_params("parallel", "parallel"),
        name="conv_in",
    )(x, g, w, bias)


HALO = 16


def _conv_out_kernel(x_ref, u_ref, ul_ref, ur_ref, wdw_ref, bdw_ref, gln_ref, bln_ref, w2_ref, b2_ref,
                     o_ref, buf_ref, c_ref, *, taps, rows):
    i = pl.program_id(1)
    n = pl.num_programs(1)
    tm = u_ref.shape[1]
    d = u_ref.shape[2]
    half = taps // 2
    nlb = d // LANES
    for cb in range(nlb):
        ls = slice(cb * LANES, (cb + 1) * LANES)
        buf_ref[cb, 0:HALO] = jnp.where(i > 0, ul_ref[0, :, ls], 0.0)
        buf_ref[cb, HALO:HALO + tm] = u_ref[0, :, ls]
        buf_ref[cb, HALO + tm:HALO + tm + HALO] = jnp.where(i < n - 1, ur_ref[0, :, ls], 0.0)

    def lane_block(cb, _):
        for r0 in range(0, tm, rows):
            acc = jnp.zeros((rows, LANES), F32)
            for t in range(taps):
                lo = r0 + HALO - half + t
                acc = acc + buf_ref[cb, lo:lo + rows] * wdw_ref[cb, t:t + 1]
            c_ref[cb, r0:r0 + rows] = acc
        return 0

    lax.fori_loop(0, nlb, lane_block, 0)
    c = jnp.concatenate([c_ref[cb] for cb in range(nlb)], axis=1) + bdw_ref[...]
    mu = jnp.mean(c, axis=-1, keepdims=True)
    cc = c - mu
    y = cc * lax.rsqrt(jnp.mean(cc * cc, axis=-1, keepdims=True) + EPS)
    y = y * gln_ref[...] + bln_ref[...]
    y = y * jax.nn.sigmoid(y)
    o_ref[0] = x_ref[0] + _dot(y.astype(BF16), w2_ref[...]) + b2_ref[...]


def _conv_out(x, u, wdw, bdw, gln, bln, w2, b2):
    b, s, d = x.shape
    taps = wdw.shape[0]
    assert taps // 2 <= HALO
    tm = _tile(s, 256)
    hb = tm // HALO
    nh = s // HALO
    tok = pl.BlockSpec((1, tm, d), lambda bi, i: (bi, i, 0))
    left = pl.BlockSpec((1, HALO, d), lambda bi, i: (bi, jnp.maximum(i * hb - 1, 0), 0))
    right = pl.BlockSpec((1, HALO, d), lambda bi, i: (bi, jnp.minimum((i + 1) * hb, nh - 1), 0))
    kern = functools.partial(_conv_out_kernel, taps=taps, rows=32)
    nlb = d // LANES
    wdw = wdw.reshape(taps, nlb, LANES).transpose(1, 0, 2)
    return pl.pallas_call(
        kern,
        grid=(b, s // tm),
        in_specs=[tok, tok, left, right, _const_spec(wdw.shape), _const_spec(bdw.shape),
                  _const_spec(gln.shape), _const_spec(bln.shape), _const_spec(w2.shape), _const_spec(b2.shape)],
        out_specs=tok,
        out_shape=jax.ShapeDtypeStruct(x.shape, F32),
        scratch_shapes=[pltpu.VMEM((nlb, tm + 2 * HALO, LANES), F32), pltpu.VMEM((nlb, tm, LANES), F32)],
        compiler_params=_params("parallel", "parallel"),
        name="conv_out",
    )(x, u, u, u, wdw, bdw, gln, bln, w2, b2)


def _mem_kv_kernel(m_ref, g_ref, w_ref, o_ref):
    h = _rms(m_ref[0], g_ref[...]).astype(BF16)
    o_ref[0] = _dot(h, w_ref[...]).astype(BF16)


def _mem_kv(mem, g, wkv):
    b, nm, d = mem.shape
    n = wkv.shape[1]
    return pl.pallas_call(
        _mem_kv_kernel,
        grid=(b,),
        in_specs=[pl.BlockSpec((1, nm, d), lambda bi: (bi, 0, 0)), _const_spec(g.shape), _const_spec(wkv.shape)],
        out_specs=pl.BlockSpec((1, nm, n), lambda bi: (bi, 0, 0)),
        out_shape=jax.ShapeDtypeStruct((b, nm, n), BF16),
        compiler_params=_params("parallel"),
        name="mem_kv",
    )(mem, g, wkv)


def _xattn_kernel(x_ref, g_ref, wq_ref, kv_ref, wo_ref, o_ref, *, scale):
    x = x_ref[0]
    d = x.shape[-1]
    hd = d // XA_HEADS
    h = _rms(x, g_ref[...]).astype(BF16)
    q = (_dot(h, wq_ref[...]) * scale).astype(BF16)
    outs = []
    for a in range(XA_HEADS):
        k = kv_ref[0, :, a * hd:(a + 1) * hd]
        v = kv_ref[0, :, d + a * hd:d + (a + 1) * hd]
        s = lax.dot_general(q[:, a * hd:(a + 1) * hd], k, (((1,), (1,)), ((), ())),
                            preferred_element_type=F32)
        p = jnp.exp(s - jnp.max(s, axis=1, keepdims=True))
        l = jnp.sum(p, axis=1, keepdims=True)
        outs.append((_dot(p.astype(BF16), v) / l).astype(BF16))
    o = jnp.concatenate(outs, axis=1)
    o_ref[0] = x + _dot(o, wo_ref[...])


def _xattn(x, g, wq, kv, wo):
    b, s, d = x.shape
    tm = _tile(s, 512)
    tok = pl.BlockSpec((1, tm, d), lambda bi, i: (bi, i, 0))
    return pl.pallas_call(
        functools.partial(_xattn_kernel, scale=(d // XA_HEADS) ** -0.5),
        grid=(b, s // tm),
        in_specs=[tok, _const_spec(g.shape), _const_spec(wq.shape),
                  pl.BlockSpec((1,) + kv.shape[1:], lambda bi, i: (bi, 0, 0)), _const_spec(wo.shape)],
        out_specs=tok,
        out_shape=jax.ShapeDtypeStruct(x.shape, F32),
        compiler_params=_params("parallel", "parallel"),
        name="mem_xattn",
    )(x, g, wq, kv, wo)


def _ffn_kernel(x_ref, g_ref, w1_ref, w3_ref, w2_ref, gf_ref, o_ref, *, final_norm):
    x = x_ref[0]
    h = _rms(x, g_ref[...]).astype(BF16)
    acc = x
    for c in range(w1_ref.shape[0]):
        a = _dot(h, w1_ref[c])
        bb = _dot(h, w3_ref[c])
        acc = acc + _dot((a * jax.nn.sigmoid(a) * bb).astype(BF16), w2_ref[c])
    if final_norm:
        acc = _rms(acc, gf_ref[...])
    o_ref[0] = acc


def _ffn(x, g, w1, w3, w2, gf, final_norm):
    b, s, d = x.shape
    tm = _tile(s, 512)
    tok = pl.BlockSpec((1, tm, d), lambda bi, i: (bi, i, 0))
    wspec = lambda w: pl.BlockSpec(w.shape, lambda bi, i: (0, 0, 0), pipeline_mode=pl.Buffered(1))
    return pl.pallas_call(
        functools.partial(_ffn_kernel, final_norm=final_norm),
        grid=(b, s // tm),
        in_specs=[tok, _const_spec(g.shape), wspec(w1), wspec(w3), wspec(w2), _const_spec(gf.shape)],
        out_specs=tok,
        out_shape=jax.ShapeDtypeStruct(x.shape, F32),
        compiler_params=_params("parallel", "parallel"),
        name="swiglu_ffn",
    )(x, g, w1, w3, w2, gf)


FF_CHUNK = 256


def _rope_tables(s):
    half = QK_ROPE // 2
    inv = ROPE_THETA ** (-jnp.arange(half, dtype=F32) / half)
    ang = jnp.arange(s, dtype=F32)[:, None] * inv[None, :]
    cos = jnp.cos(ang)
    sin = jnp.sin(ang)
    ones = jnp.ones((s, QK_NOPE), F32)
    tail = HEAD_PAD - QK_NOPE - QK_ROPE
    cos_t = jnp.concatenate([ones, cos, cos, jnp.ones((s, tail), F32)], axis=1)
    sin_t = jnp.concatenate([0 * ones, -sin, sin, jnp.zeros((s, tail), F32)], axis=1)
    return cos_t, sin_t


def _swap_halves(w):
    half = QK_ROPE // 2
    return jnp.concatenate([w[..., half:], w[..., :half]], axis=-1)


def _prep_mixer(w_in, w_uq, w_ukv, w_out):
    d = w_in.shape[0]
    q_rank = w_uq.shape[0]
    kv_rank = w_ukv.shape[0]
    f_w = FNET_GROUPS * FNET_CH
    tail = HEAD_PAD - QK_NOPE - QK_ROPE
    o = q_rank + kv_rank
    w_kr = w_in[:, o:o + QK_ROPE]
    pad_rope = lambda w: jnp.concatenate(
        [jnp.zeros(w.shape[:-1] + (QK_NOPE,), F32), w, jnp.zeros(w.shape[:-1] + (tail,), F32)], axis=-1)
    win = jnp.concatenate([w_in[:, :o], w_in[:, o + QK_ROPE:], pad_rope(w_kr), pad_rope(_swap_halves(w_kr))],
                          axis=1).astype(BF16)
    uq = w_uq.reshape(q_rank, MLA_HEADS, QK_NOPE + QK_ROPE)
    zt = jnp.zeros((q_rank, MLA_HEADS, tail), F32)
    wuq = jnp.concatenate([uq, zt], axis=-1).reshape(q_rank, -1).astype(BF16)
    wuqr = jnp.concatenate([jnp.zeros((q_rank, MLA_HEADS, QK_NOPE), F32), _swap_halves(uq[..., QK_NOPE:]), zt],
                           axis=-1).reshape(q_rank, -1).astype(BF16)
    ukv = w_ukv.reshape(kv_rank, MLA_HEADS, QK_NOPE + V_HEAD)
    wuk = jnp.concatenate([ukv[..., :QK_NOPE], jnp.zeros((kv_rank, MLA_HEADS, HEAD_PAD - QK_NOPE), F32)],
                          axis=-1).reshape(kv_rank, -1).astype(BF16)
    wuv = ukv[..., QK_NOPE:].reshape(kv_rank, -1).astype(BF16)
    av = MLA_HEADS * V_HEAD
    return dict(win=win, wuq=wuq, wuqr=wuqr, wuk=wuk, wuv=wuv,
                wa=w_out[:av].astype(BF16), wf=w_out[av:].astype(BF16),
                q_rank=q_rank, kv_rank=kv_rank, f_w=f_w)


def _row(v):
    return v.reshape(1, -1).astype(F32)


def _trunk(x, mem, p):
    depth = p['g_mix'].shape[0]
    s = x.shape[1]
    for i in range(depth):
        j = i // 2
        if i % 2 == 0:
            mp = p['mixer'][j]
            cos_t, sin_t = _rope_tables(s)
            q_scale = (QK_NOPE + QK_ROPE) ** -0.5
            q, k, v, f = _mixer_in(x, _row(p['g_mix'][i]), mp['win'], _row(p['mla_g_q'][j]), mp['wuq'], mp['wuqr'],
                                   _row(p['mla_g_kv'][j]), mp['wuk'], mp['wuv'], cos_t, sin_t,
                                   q_rank=mp['q_rank'], kv_rank=mp['kv_rank'], f_w=mp['f_w'], q_scale=q_scale)
            attn = _flash(q, k, v)
            fo = _fnet(f)
            x = _mixer_out(x, attn, fo, mp['wa'], mp['wf'])
        else:
            u = _conv_in(x, _row(p['g_mix'][i]), p['conv_w_pw1'][j], _row(p['conv_b_pw1'][j]))
            x = _conv_out(x, u, p['conv_w_dw'][j], _row(p['conv_b_dw'][j]), _row(p['conv_g_ln'][j]),
                          _row(p['conv_b_ln'][j]), p['conv_w_pw2'][j], _row(p['conv_b_pw2'][j]))
        kv = _mem_kv(mem, _row(p['g_xkv'][i]), p['xa_wkv'][i])
        x = _xattn(x, _row(p['g_xq'][i]), p['xa_wq'][i], kv, p['xa_wo'][i])
        x = _ffn(x, _row(p['g_ffn'][i]), p['ffn_w1'][i], p['ffn_w3'][i], p['ffn_w2'][i],
                 _row(p['g_final']), final_norm=(i == depth - 1))
    return x


def kernel(x_prompt, x_sample, mem_prompt, mem_sample, g_mix, g_xq, g_xkv, xa_wq, xa_wk, xa_wv, xa_wo, g_ffn, ffn_w1, ffn_w3, ffn_w2, ab_w_in, mla_g_q, mla_w_uq, mla_g_kv, mla_w_ukv, ab_w_out, conv_w_pw1, conv_b_pw1, conv_w_dw, conv_b_dw, conv_g_ln, conv_b_ln, conv_w_pw2, conv_b_pw2, g_final):
    depth, d, d_ff = ffn_w1.shape
    nc = d_ff // FF_CHUNK
    assert nc * FF_CHUNK == d_ff
    p = {
        'g_mix': g_mix, 'g_xq': g_xq, 'g_xkv': g_xkv, 'g_ffn': g_ffn, 'g_final': g_final,
        'xa_wq': xa_wq.astype(BF16), 'xa_wo': xa_wo.astype(BF16),
        'xa_wkv': jnp.concatenate([xa_wk, xa_wv], axis=-1).astype(BF16),
        'ffn_w1': ffn_w1.astype(BF16).reshape(depth, d, nc, FF_CHUNK).transpose(0, 2, 1, 3),
        'ffn_w3': ffn_w3.astype(BF16).reshape(depth, d, nc, FF_CHUNK).transpose(0, 2, 1, 3),
        'ffn_w2': ffn_w2.astype(BF16).reshape(depth, nc, FF_CHUNK, d),
        'mla_g_q': mla_g_q, 'mla_g_kv': mla_g_kv,
        'mixer': [_prep_mixer(ab_w_in[j], mla_w_uq[j], mla_w_ukv[j], ab_w_out[j])
                  for j in range(ab_w_in.shape[0])],
        'conv_w_pw1': conv_w_pw1.astype(BF16), 'conv_b_pw1': conv_b_pw1,
        'conv_w_dw': conv_w_dw, 'conv_b_dw': conv_b_dw, 'conv_g_ln': conv_g_ln, 'conv_b_ln': conv_b_ln,
        'conv_w_pw2': conv_w_pw2.astype(BF16), 'conv_b_pw2': conv_b_pw2,
    }
    return (_trunk(x_prompt, mem_prompt, p), _trunk(x_sample, mem_sample, p))
```

```python
import functools
import math

import jax
import jax.numpy as jnp
from jax import lax
from jax.experimental import pallas as pl
from jax.experimental.pallas import tpu as pltpu

MLA_HEADS = 8
QK_NOPE = 64
QK_ROPE = 32
V_HEAD = 64
ROPE_THETA = 10000.0
FNET_GROUPS = 4
FNET_CH = 128
XA_HEADS = 4
EPS = 1e-6

LANES = 128
HEAD_PAD = 128
VT_ROWS = 80
KV_CHUNK = 512
VMEM_LIMIT = 52 * 1024 * 1024

F32 = jnp.float32
BF16 = jnp.bfloat16


def _params(*sem):
    return pltpu.CompilerParams(dimension_semantics=sem, vmem_limit_bytes=VMEM_LIMIT)


def _rms(x, g):
    ms = jnp.mean(x * x, axis=-1, keepdims=True)
    return x * lax.rsqrt(ms + EPS) * g


def _dot(a, b):
    return jnp.dot(a, b, preferred_element_type=F32)


def _dot_exact(a, b):
    return jnp.dot(a, b, preferred_element_type=F32, precision=lax.Precision.HIGHEST)


def _const_spec(shape):
    nd = len(shape)
    return pl.BlockSpec(shape, lambda *_: (0,) * nd)


def _tile(n, want):
    t = min(n, want)
    assert n % t == 0, (n, t)
    return t


def _mixer_in_kernel(x_ref, g_ref, win_ref, gq_ref, wuq_ref, wuqr_ref, gkv_ref, wuk_ref, wuvt_ref, ones_ref,
                     cos_ref, sin_ref, q_ref, k_ref, vt_ref, f_ref, *, q_rank, kv_rank, f_w, q_scale):
    x = x_ref[0]
    h = _rms(x, g_ref[...]).astype(BF16)
    z = _dot(h, win_ref[...])
    o = 0
    cq = z[:, o:o + q_rank]; o += q_rank
    ckv = z[:, o:o + kv_rank]; o += kv_rank
    f_ref[0] = z[:, o:o + f_w]; o += f_w
    kr = z[:, o:o + HEAD_PAD]; o += HEAD_PAD
    krr = z[:, o:o + HEAD_PAD]
    cos = cos_ref[...]
    sin = sin_ref[...]
    cqn = _rms(cq, gq_ref[...]).astype(BF16)
    q = _dot(cqn, wuq_ref[...])
    qr = _dot(cqn, wuqr_ref[...])
    ckvn = _rms(ckv, gkv_ref[...]).astype(BF16)
    k = _dot(ckvn, wuk_ref[...])
    vt = lax.dot_general(wuvt_ref[...], ckvn, (((1,), (1,)), ((), ())), preferred_element_type=F32)
    vt = vt + ones_ref[...]
    for hd in range(MLA_HEADS):
        vt_ref[0, hd, 0] = vt[hd * VT_ROWS:(hd + 1) * VT_ROWS].astype(BF16)
    k_rope = kr * cos + krr * sin
    for hd in range(MLA_HEADS):
        sl = slice(hd * HEAD_PAD, (hd + 1) * HEAD_PAD)
        q_ref[0, :, sl] = ((q[:, sl] * cos + qr[:, sl] * sin) * q_scale).astype(BF16)
        k_ref[0, :, sl] = (k[:, sl] + k_rope).astype(BF16)


def _mixer_in(x, g, win, gq, wuq, wuqr, gkv, wuk, wuvt, ones, cos, sin, *, q_rank, kv_rank, f_w, q_scale):
    b, s, d = x.shape
    tm = _tile(s, KV_CHUNK)
    hw = MLA_HEADS * HEAD_PAD
    tok = lambda w: pl.BlockSpec((1, tm, w), lambda bi, i: (bi, i, 0))
    rope = pl.BlockSpec((tm, HEAD_PAD), lambda bi, i: (i, 0))
    kern = functools.partial(_mixer_in_kernel, q_rank=q_rank, kv_rank=kv_rank, f_w=f_w, q_scale=q_scale)
    return pl.pallas_call(
        kern,
        grid=(b, s // tm),
        in_specs=[tok(d), _const_spec(g.shape), _const_spec(win.shape), _const_spec(gq.shape),
                  _const_spec(wuq.shape), _const_spec(wuqr.shape), _const_spec(gkv.shape),
                  _const_spec(wuk.shape), _const_spec(wuvt.shape), _const_spec(ones.shape), rope, rope],
        out_specs=[tok(hw), tok(hw),
                   pl.BlockSpec((1, MLA_HEADS, 1, VT_ROWS, tm), lambda bi, i: (bi, 0, i, 0, 0)),
                   tok(f_w)],
        out_shape=[jax.ShapeDtypeStruct((b, s, hw), BF16), jax.ShapeDtypeStruct((b, s, hw), BF16),
                   jax.ShapeDtypeStruct((b, MLA_HEADS, s // tm, VT_ROWS, tm), BF16),
                   jax.ShapeDtypeStruct((b, s, f_w), F32)],
        compiler_params=_params("parallel", "parallel"),
        name="mixer_in",
    )(x, g, win, gq, wuq, wuqr, gkv, wuk, wuvt, ones, cos, sin)


def _flash_kernel(q_ref, k_ref, vt_ref, o_ref, s_ref):
    tq = q_ref.shape[1]
    nk, _, tk = vt_ref.shape[2:]
    q = q_ref[0]

    def scores(j, slot):
        k = k_ref[0, pl.ds(pl.multiple_of(j * tk, tk), tk), :]
        s = lax.dot_general(k, q, (((1,), (1,)), ((), ())), preferred_element_type=F32)
        s_ref[slot] = s
        return jnp.max(s, axis=0, keepdims=True)

    def update(j, slot, m, acc, cmax):
        m_new = jnp.maximum(m, cmax)
        alpha = jnp.exp2(m - m_new)
        p = jnp.exp2(s_ref[slot] - m_new).astype(BF16)
        return m_new, alpha * acc + _dot(vt_ref[0, 0, j], p)

    m = jnp.full((1, tq), -jnp.inf, F32)
    acc = jnp.zeros((VT_ROWS, tq), F32)
    c0 = scores(0, 0)
    if nk > 1:
        assert nk % 2 == 0, nk

        def body(t, carry):
            m, acc, c0 = carry
            c1 = scores(2 * t + 1, 1)
            m, acc = update(2 * t, 0, m, acc, c0)
            c0 = scores(2 * t + 2, 0)
            m, acc = update(2 * t + 1, 1, m, acc, c1)
            return m, acc, c0

        m, acc, c0 = lax.fori_loop(0, nk // 2 - 1, body, (m, acc, c0))
        c1 = scores(nk - 1, 1)
        m, acc = update(nk - 2, 0, m, acc, c0)
        m, acc = update(nk - 1, 1, m, acc, c1)
    else:
        m, acc = update(0, 0, m, acc, c0)
    o_ref[0] = (acc[:V_HEAD] / acc[V_HEAD:V_HEAD + 1]).astype(BF16)


def _flash(q, k, vt):
    b, s, _ = q.shape
    tq = _tile(s, 512)
    _, _, nk, _, tk = vt.shape
    return pl.pallas_call(
        _flash_kernel,
        grid=(b, MLA_HEADS, s // tq),
        in_specs=[pl.BlockSpec((1, tq, HEAD_PAD), lambda bi, h, i: (bi, i, h)),
                  pl.BlockSpec((1, s, HEAD_PAD), lambda bi, h, i: (bi, 0, h)),
                  pl.BlockSpec((1, 1, nk, VT_ROWS, tk), lambda bi, h, i: (bi, h, 0, 0, 0))],
        out_specs=pl.BlockSpec((1, V_HEAD, tq), lambda bi, h, i: (bi, h, i)),
        out_shape=jax.ShapeDtypeStruct((b, MLA_HEADS * V_HEAD, s), BF16),
        scratch_shapes=[pltpu.VMEM((2, tk, tq), F32)],
        compiler_params=_params("parallel", "parallel", "arbitrary"),
        name="mla_flash",
    )(q, k, vt)


def _dft_a_kernel(x_ref, c_ref, s_ref, ar_ref, ai_ref):
    x = x_ref[0]
    ar_ref[0] = _dot_exact(c_ref[...], x)
    ai_ref[0] = -_dot_exact(s_ref[...], x)


def _dft_a(x2, c1, s1):
    b, n1, w = x2.shape
    tc = _tile(w, 4096)
    blk = pl.BlockSpec((1, n1, tc), lambda bi, j: (bi, 0, j))
    sds = jax.ShapeDtypeStruct((b, n1, w), F32)
    return pl.pallas_call(
        _dft_a_kernel,
        grid=(b, w // tc),
        in_specs=[blk, _const_spec(c1.shape), _const_spec(s1.shape)],
        out_specs=[blk, blk],
        out_shape=[sds, sds],
        compiler_params=_params("parallel", "parallel"),
        name="fnet_dft_a",
    )(x2, c1, s1)


def _dft_b_kernel(ar_ref, ai_ref, gr_ref, gi_ref, cc_ref, sc_ref, o_ref, *, kb, scale):
    cc = cc_ref[...]
    sc = sc_ref[...]
    for j in range(kb):
        ar = ar_ref[0, j]
        ai = ai_ref[0, j]
        gr = gr_ref[j]
        gi = gi_ref[j]
        yr = _dot_exact(gr, ar) - _dot_exact(gi, ai)
        yi = _dot_exact(gr, ai) + _dot_exact(gi, ar)
        for grp in range(FNET_GROUPS):
            sl = slice(grp * FNET_CH, (grp + 1) * FNET_CH)
            res = _dot_exact(yr[:, sl], cc) + _dot_exact(yi[:, sl], sc)
            o_ref[0, :, j, sl] = res * scale


def _dft_b(ar, ai, gr, gi, cc, sc, scale):
    b, n1, n2, w = ar.shape
    kb = 8
    assert n1 % kb == 0
    a_blk = pl.BlockSpec((1, kb, n2, w), lambda bi, i: (bi, i, 0, 0))
    g_blk = pl.BlockSpec((kb, n2, n2), lambda bi, i: (i, 0, 0))
    return pl.pallas_call(
        functools.partial(_dft_b_kernel, kb=kb, scale=scale),
        grid=(b, n1 // kb),
        in_specs=[a_blk, a_blk, g_blk, g_blk, _const_spec(cc.shape), _const_spec(sc.shape)],
        out_specs=pl.BlockSpec((1, n2, kb, w), lambda bi, i: (bi, 0, i, 0)),
        out_shape=jax.ShapeDtypeStruct((b, n2, n1, w), F32),
        compiler_params=_params("parallel", "parallel"),
        name="fnet_dft_b",
    )(ar, ai, gr, gi, cc, sc)


def _dft_tables(s):
    n1 = 1 << ((s.bit_length() - 1 + 1) // 2)
    n2 = s // n1
    assert n1 * n2 == s and n1 % 8 == 0, (s, n1, n2)

    def cs(idx, period):
        ang = (2.0 * math.pi / period) * (idx % period).astype(F32)
        return jnp.cos(ang), jnp.sin(ang)

    i1 = jnp.arange(n1, dtype=jnp.int32)
    i2 = jnp.arange(n2, dtype=jnp.int32)
    c1, s1 = cs(i1[:, None] * i1[None, :], n1)
    idx = (i2[None, None, :] * (i1[:, None, None] + n1 * i2[None, :, None])) % s
    gr, gi_pos = cs(idx, s)
    ic = jnp.arange(FNET_CH, dtype=jnp.int32)
    cc, sc = cs(ic[:, None] * ic[None, :], FNET_CH)
    return n1, n2, c1, s1, gr, -gi_pos, cc, sc


def _fnet(f):
    b, s, w = f.shape
    n1, n2, c1, s1, gr, gi, cc, sc = _dft_tables(s)
    ar, ai = _dft_a(f.reshape(b, n1, n2 * w), c1, s1)
    out = _dft_b(ar.reshape(b, n1, n2, w), ai.reshape(b, n1, n2, w), gr, gi, cc, sc,
                 1.0 / math.sqrt(s * FNET_CH))
    return out.reshape(b, s, w)


def _mixer_out_kernel(x_ref, at_ref, f_ref, wa_ref, wf_ref, o_ref):
    attn = lax.dot_general(at_ref[0], wa_ref[...], (((0,), (0,)), ((), ())), preferred_element_type=F32)
    o_ref[0] = x_ref[0] + attn + _dot(f_ref[0].astype(BF16), wf_ref[...])


def _mixer_out(x, attn_t, fo, wa, wf):
    b, s, d = x.shape
    tm = _tile(s, 512)
    tok = lambda w: pl.BlockSpec((1, tm, w), lambda bi, i: (bi, i, 0))
    return pl.pallas_call(
        _mixer_out_kernel,
        grid=(b, s // tm),
        in_specs=[tok(d), pl.BlockSpec((1, attn_t.shape[1], tm), lambda bi, i: (bi, 0, i)),
                  tok(fo.shape[-1]), _const_spec(wa.shape), _const_spec(wf.shape)],
        out_specs=tok(d),
        out_shape=jax.ShapeDtypeStruct(x.shape, F32),
        compiler_params=_params("parallel", "parallel"),
        name="mixer_out",
    )(x, attn_t, fo, wa, wf)


def _conv_in_kernel(x_ref, g_ref, w_ref, b_ref, u_ref):
    d = x_ref.shape[-1]
    h = _rms(x_ref[0], g_ref[...]).astype(BF16)
    z = _dot(h, w_ref[...]) + b_ref[...]
    u_ref[0] = z[:, :d] * jax.nn.sigmoid(z[:, d:])


def _conv_in(x, g, w, bias):
    b, s, d = x.shape
    tm = _tile(s, 512)
    tok = pl.BlockSpec((1, tm, d), lambda bi, i: (bi, i, 0))
    return pl.pallas_call(
        _conv_in_kernel,
        grid=(b, s // tm),
        in_specs=[tok, _const_spec(g.shape), _const_spec(w.shape), _const_spec(bias.shape)],
        out_specs=tok,
        out_shape=jax.ShapeDtypeStruct(x.shape, F32),
        compiler_params=_params("parallel", "parallel"),
        name="conv_in",
    )(x, g, w, bias)


HALO = 16


def _conv_out_kernel(x_ref, u_ref, ul_ref, ur_ref, wdw_ref, bdw_ref, gln_ref, bln_ref, w2_ref, b2_ref,
                     o_ref, buf_ref, c_ref, *, taps, rows):
    i = pl.program_id(1)
    n = pl.num_programs(1)
    tm = u_ref.shape[1]
    d = u_ref.shape[2]
    half = taps // 2
    nlb = d // LANES
    for cb in range(nlb):
        ls = slice(cb * LANES, (cb + 1) * LANES)
        buf_ref[cb, 0:HALO] = jnp.where(i > 0, ul_ref[0, :, ls], 0.0)
        buf_ref[cb, HALO:HALO + tm] = u_ref[0, :, ls]
        buf_ref[cb, HALO + tm:HALO + tm + HALO] = jnp.where(i < n - 1, ur_ref[0, :, ls], 0.0)

    def lane_block(cb, _):
        for r0 in range(0, tm, rows):
            acc = jnp.zeros((rows, LANES), F32)
            for t in range(taps):
                lo = r0 + HALO - half + t
                acc = acc + buf_ref[cb, lo:lo + rows] * wdw_ref[cb, t:t + 1]
            c_ref[cb, r0:r0 + rows] = acc
        return 0

    lax.fori_loop(0, nlb, lane_block, 0)
    c = jnp.concatenate([c_ref[cb] for cb in range(nlb)], axis=1) + bdw_ref[...]
    mu = jnp.mean(c, axis=-1, keepdims=True)
    cc = c - mu
    y = cc * lax.rsqrt(jnp.mean(cc * cc, axis=-1, keepdims=True) + EPS)
    y = y * gln_ref[...] + bln_ref[...]
    y = y * jax.nn.sigmoid(y)
    o_ref[0] = x_ref[0] + _dot(y.astype(BF16), w2_ref[...]) + b2_ref[...]


def _conv_out(x, u, wdw, bdw, gln, bln, w2, b2):
    b, s, d = x.shape
    taps = wdw.shape[0]
    assert taps // 2 <= HALO
    tm = _tile(s, 256)
    hb = tm // HALO
    nh = s // HALO
    tok = pl.BlockSpec((1, tm, d), lambda bi, i: (bi, i, 0))
    left = pl.BlockSpec((1, HALO, d), lambda bi, i: (bi, jnp.maximum(i * hb - 1, 0), 0))
    right = pl.BlockSpec((1, HALO, d), lambda bi, i: (bi, jnp.minimum((i + 1) * hb, nh - 1), 0))
    kern = functools.partial(_conv_out_kernel, taps=taps, rows=32)
    nlb = d // LANES
    wdw = wdw.reshape(taps, nlb, LANES).transpose(1, 0, 2)
    return pl.pallas_call(
        kern,
        grid=(b, s // tm),
        in_specs=[tok, tok, left, right, _const_spec(wdw.shape), _const_spec(bdw.shape),
                  _const_spec(gln.shape), _const_spec(bln.shape), _const_spec(w2.shape), _const_spec(b2.shape)],
        out_specs=tok,
        out_shape=jax.ShapeDtypeStruct(x.shape, F32),
        scratch_shapes=[pltpu.VMEM((nlb, tm + 2 * HALO, LANES), F32), pltpu.VMEM((nlb, tm, LANES), F32)],
        compiler_params=_params("parallel", "parallel"),
        name="conv_out",
    )(x, u, u, u, wdw, bdw, gln, bln, w2, b2)


def _mem_kv_kernel(m_ref, g_ref, w_ref, o_ref):
    h = _rms(m_ref[0], g_ref[...]).astype(BF16)
    o_ref[0] = _dot(h, w_ref[...]).astype(BF16)


def _mem_kv(mem, g, wkv):
    b, nm, d = mem.shape
    n = wkv.shape[1]
    return pl.pallas_call(
        _mem_kv_kernel,
        grid=(b,),
        in_specs=[pl.BlockSpec((1, nm, d), lambda bi: (bi, 0, 0)), _const_spec(g.shape), _const_spec(wkv.shape)],
        out_specs=pl.BlockSpec((1, nm, n), lambda bi: (bi, 0, 0)),
        out_shape=jax.ShapeDtypeStruct((b, nm, n), BF16),
        compiler_params=_params("parallel"),
        name="mem_kv",
    )(mem, g, wkv)


def _xattn_kernel(x_ref, g_ref, wq_ref, kv_ref, wo_ref, o_ref, *, scale):
    x = x_ref[0]
    d = x.shape[-1]
    hd = d // XA_HEADS
    h = _rms(x, g_ref[...]).astype(BF16)
    q = (_dot(h, wq_ref[...]) * scale).astype(BF16)
    outs = []
    for a in range(XA_HEADS):
        k = kv_ref[0, :, a * hd:(a + 1) * hd]
        v = kv_ref[0, :, d + a * hd:d + (a + 1) * hd]
        s = lax.dot_general(q[:, a * hd:(a + 1) * hd], k, (((1,), (1,)), ((), ())),
                            preferred_element_type=F32)
        p = jnp.exp(s - jnp.max(s, axis=1, keepdims=True))
        l = jnp.sum(p, axis=1, keepdims=True)
        outs.append((_dot(p.astype(BF16), v) / l).astype(BF16))
    o = jnp.concatenate(outs, axis=1)
    o_ref[0] = x + _dot(o, wo_ref[...])


def _xattn(x, g, wq, kv, wo):
    b, s, d = x.shape
    tm = _tile(s, 512)
    tok = pl.BlockSpec((1, tm, d), lambda bi, i: (bi, i, 0))
    return pl.pallas_call(
        functools.partial(_xattn_kernel, scale=(d // XA_HEADS) ** -0.5),
        grid=(b, s // tm),
        in_specs=[tok, _const_spec(g.shape), _const_spec(wq.shape),
                  pl.BlockSpec((1,) + kv.shape[1:], lambda bi, i: (bi, 0, 0)), _const_spec(wo.shape)],
        out_specs=tok,
        out_shape=jax.ShapeDtypeStruct(x.shape, F32),
        compiler_params=_params("parallel", "parallel"),
        name="mem_xattn",
    )(x, g, wq, kv, wo)


def _ffn_kernel(x_ref, g_ref, w1_ref, w3_ref, w2_ref, gf_ref, o_ref, *, final_norm):
    x = x_ref[0]
    h = _rms(x, g_ref[...]).astype(BF16)
    acc = x
    for c in range(w1_ref.shape[0]):
        a = _dot(h, w1_ref[c])
        bb = _dot(h, w3_ref[c])
        acc = acc + _dot((a * jax.nn.sigmoid(a) * bb).astype(BF16), w2_ref[c])
    if final_norm:
        acc = _rms(acc, gf_ref[...])
    o_ref[0] = acc


def _ffn(x, g, w1, w3, w2, gf, final_norm):
    b, s, d = x.shape
    tm = _tile(s, 512)
    tok = pl.BlockSpec((1, tm, d), lambda bi, i: (bi, i, 0))
    wspec = lambda w: pl.BlockSpec(w.shape, lambda bi, i: (0, 0, 0), pipeline_mode=pl.Buffered(1))
    return pl.pallas_call(
        functools.partial(_ffn_kernel, final_norm=final_norm),
        grid=(b, s // tm),
        in_specs=[tok, _const_spec(g.shape), wspec(w1), wspec(w3), wspec(w2), _const_spec(gf.shape)],
        out_specs=tok,
        out_shape=jax.ShapeDtypeStruct(x.shape, F32),
        compiler_params=_params("parallel", "parallel"),
        name="swiglu_ffn",
    )(x, g, w1, w3, w2, gf)


FF_CHUNK = 256


def _rope_tables(s):
    half = QK_ROPE // 2
    inv = ROPE_THETA ** (-jnp.arange(half, dtype=F32) / half)
    ang = jnp.arange(s, dtype=F32)[:, None] * inv[None, :]
    cos = jnp.cos(ang)
    sin = jnp.sin(ang)
    ones = jnp.ones((s, QK_NOPE), F32)
    tail = HEAD_PAD - QK_NOPE - QK_ROPE
    cos_t = jnp.concatenate([ones, cos, cos, jnp.ones((s, tail), F32)], axis=1)
    sin_t = jnp.concatenate([0 * ones, -sin, sin, jnp.zeros((s, tail), F32)], axis=1)
    return cos_t, sin_t


def _swap_halves(w):
    half = QK_ROPE // 2
    return jnp.concatenate([w[..., half:], w[..., :half]], axis=-1)


def _prep_mixer(w_in, w_uq, w_ukv, w_out):
    d = w_in.shape[0]
    q_rank = w_uq.shape[0]
    kv_rank = w_ukv.shape[0]
    f_w = FNET_GROUPS * FNET_CH
    tail = HEAD_PAD - QK_NOPE - QK_ROPE
    o = q_rank + kv_rank
    w_kr = w_in[:, o:o + QK_ROPE]
    pad_rope = lambda w: jnp.concatenate(
        [jnp.zeros(w.shape[:-1] + (QK_NOPE,), F32), w, jnp.zeros(w.shape[:-1] + (tail,), F32)], axis=-1)
    win = jnp.concatenate([w_in[:, :o], w_in[:, o + QK_ROPE:], pad_rope(w_kr), pad_rope(_swap_halves(w_kr))],
                          axis=1).astype(BF16)
    uq = w_uq.reshape(q_rank, MLA_HEADS, QK_NOPE + QK_ROPE)
    zt = jnp.zeros((q_rank, MLA_HEADS, tail), F32)
    wuq = jnp.concatenate([uq, zt], axis=-1).reshape(q_rank, -1).astype(BF16)
    wuqr = jnp.concatenate([jnp.zeros((q_rank, MLA_HEADS, QK_NOPE), F32), _swap_halves(uq[..., QK_NOPE:]), zt],
                           axis=-1).reshape(q_rank, -1).astype(BF16)
    ukv = w_ukv.reshape(kv_rank, MLA_HEADS, QK_NOPE + V_HEAD)
    wuk = jnp.concatenate([ukv[..., :QK_NOPE], jnp.zeros((kv_rank, MLA_HEADS, HEAD_PAD - QK_NOPE), F32)],
                          axis=-1).reshape(kv_rank, -1).astype(BF16)
    wuvt = jnp.concatenate([ukv[..., QK_NOPE:], jnp.zeros((kv_rank, MLA_HEADS, VT_ROWS - V_HEAD), F32)],
                           axis=-1).reshape(kv_rank, -1).T.astype(BF16)
    ones = (jnp.arange(MLA_HEADS * VT_ROWS) % VT_ROWS == V_HEAD).astype(F32).reshape(-1, 1)
    av = MLA_HEADS * V_HEAD
    return dict(win=win, wuq=wuq, wuqr=wuqr, wuk=wuk, wuvt=wuvt, ones=ones,
                wa=w_out[:av].astype(BF16), wf=w_out[av:].astype(BF16),
                q_rank=q_rank, kv_rank=kv_rank, f_w=f_w)


def _row(v):
    return v.reshape(1, -1).astype(F32)


def _trunk(x, mem, p):
    depth = p['g_mix'].shape[0]
    s = x.shape[1]
    for i in range(depth):
        j = i // 2
        if i % 2 == 0:
            mp = p['mixer'][j]
            cos_t, sin_t = _rope_tables(s)
            q_scale = (QK_NOPE + QK_ROPE) ** -0.5 * math.log2(math.e)
            q, k, vt, f = _mixer_in(x, _row(p['g_mix'][i]), mp['win'], _row(p['mla_g_q'][j]), mp['wuq'], mp['wuqr'],
                                    _row(p['mla_g_kv'][j]), mp['wuk'], mp['wuvt'], mp['ones'], cos_t, sin_t,
                                    q_rank=mp['q_rank'], kv_rank=mp['kv_rank'], f_w=mp['f_w'], q_scale=q_scale)
            attn_t = _flash(q, k, vt)
            fo = _fnet(f)
            x = _mixer_out(x, attn_t, fo, mp['wa'], mp['wf'])
        else:
            u = _conv_in(x, _row(p['g_mix'][i]), p['conv_w_pw1'][j], _row(p['conv_b_pw1'][j]))
            x = _conv_out(x, u, p['conv_w_dw'][j], _row(p['conv_b_dw'][j]), _row(p['conv_g_ln'][j]),
                          _row(p['conv_b_ln'][j]), p['conv_w_pw2'][j], _row(p['conv_b_pw2'][j]))
        kv = _mem_kv(mem, _row(p['g_xkv'][i]), p['xa_wkv'][i])
        x = _xattn(x, _row(p['g_xq'][i]), p['xa_wq'][i], kv, p['xa_wo'][i])
        x = _ffn(x, _row(p['g_ffn'][i]), p['ffn_w1'][i], p['ffn_w3'][i], p['ffn_w2'][i],
                 _row(p['g_final']), final_norm=(i == depth - 1))
    return x


def kernel(x_prompt, x_sample, mem_prompt, mem_sample, g_mix, g_xq, g_xkv, xa_wq, xa_wk, xa_wv, xa_wo, g_ffn, ffn_w1, ffn_w3, ffn_w2, ab_w_in, mla_g_q, mla_w_uq, mla_g_kv, mla_w_ukv, ab_w_out, conv_w_pw1, conv_b_pw1, conv_w_dw, conv_b_dw, conv_g_ln, conv_b_ln, conv_w_pw2, conv_b_pw2, g_final):
    depth, d, d_ff = ffn_w1.shape
    nc = d_ff // FF_CHUNK
    assert nc * FF_CHUNK == d_ff
    p = {
        'g_mix': g_mix, 'g_xq': g_xq, 'g_xkv': g_xkv, 'g_ffn': g_ffn, 'g_final': g_final,
        'xa_wq': xa_wq.astype(BF16), 'xa_wo': xa_wo.astype(BF16),
        'xa_wkv': jnp.concatenate([xa_wk, xa_wv], axis=-1).astype(BF16),
        'ffn_w1': ffn_w1.astype(BF16).reshape(depth, d, nc, FF_CHUNK).transpose(0, 2, 1, 3),
        'ffn_w3': ffn_w3.astype(BF16).reshape(depth, d, nc, FF_CHUNK).transpose(0, 2, 1, 3),
        'ffn_w2': ffn_w2.astype(BF16).reshape(depth, nc, FF_CHUNK, d),
        'mla_g_q': mla_g_q, 'mla_g_kv': mla_g_kv,
        'mixer': [_prep_mixer(ab_w_in[j], mla_w_uq[j], mla_w_ukv[j], ab_w_out[j])
                  for j in range(ab_w_in.shape[0])],
        'conv_w_pw1': conv_w_pw1.astype(BF16), 'conv_b_pw1': conv_b_pw1,
        'conv_w_dw': conv_w_dw, 'conv_b_dw': conv_b_dw, 'conv_g_ln': conv_g_ln, 'conv_b_ln': conv_b_ln,
        'conv_w_pw2': conv_w_pw2.astype(BF16), 'conv_b_pw2': conv_b_pw2,
    }
    return (_trunk(x_prompt, mem_prompt, p), _trunk(x_sample, mem_sample, p))
```

```python
import functools
import math

import jax
import jax.numpy as jnp
from jax import lax
from jax.experimental import pallas as pl
from jax.experimental.pallas import tpu as pltpu

MLA_HEADS = 8
QK_NOPE = 64
QK_ROPE = 32
V_HEAD = 64
ROPE_THETA = 10000.0
FNET_GROUPS = 4
FNET_CH = 128
XA_HEADS = 4
EPS = 1e-6

LANES = 128
HEAD_PAD = 128
VT_ROWS = 80
KV_CHUNK = 512
SLOTS = 8
FF_CHUNK = 256
VMEM_LIMIT = 52 * 1024 * 1024

F32 = jnp.float32
BF16 = jnp.bfloat16


def _params(*sem):
    return pltpu.CompilerParams(dimension_semantics=sem, vmem_limit_bytes=VMEM_LIMIT)


def _rms(x, g):
    ms = jnp.mean(x * x, axis=-1, keepdims=True)
    return x * lax.rsqrt(ms + EPS) * g


def _dot(a, b):
    return jnp.dot(a, b, preferred_element_type=F32)


def _const_spec(shape):
    nd = len(shape)
    return pl.BlockSpec(shape, lambda *_: (0,) * nd)


def _tile(n, want):
    t = min(n, want)
    assert n % t == 0, (n, t)
    return t


def _mixer_in_kernel(x_ref, g_ref, win_ref, gq_ref, wuq_ref, wuqr_ref, gkv_ref, wuk_ref, wuvt_ref, ones_ref,
                     cos_ref, sin_ref, q_ref, k_ref, vt_ref, f_ref, *, q_rank, kv_rank, f_w, q_scale):
    x = x_ref[0]
    h = _rms(x, g_ref[...]).astype(BF16)
    z = _dot(h, win_ref[...])
    o = 0
    cq = z[:, o:o + q_rank]; o += q_rank
    ckv = z[:, o:o + kv_rank]; o += kv_rank
    f_ref[0] = z[:, o:o + f_w]; o += f_w
    kr = z[:, o:o + HEAD_PAD]; o += HEAD_PAD
    krr = z[:, o:o + HEAD_PAD]
    cos = cos_ref[...]
    sin = sin_ref[...]
    cqn = _rms(cq, gq_ref[...]).astype(BF16)
    q = _dot(cqn, wuq_ref[...])
    qr = _dot(cqn, wuqr_ref[...])
    ckvn = _rms(ckv, gkv_ref[...]).astype(BF16)
    k = _dot(ckvn, wuk_ref[...])
    vt = lax.dot_general(wuvt_ref[...], ckvn, (((1,), (1,)), ((), ())), preferred_element_type=F32)
    vt = vt + ones_ref[...]
    for hd in range(MLA_HEADS):
        vt_ref[0, hd, 0] = vt[hd * VT_ROWS:(hd + 1) * VT_ROWS].astype(BF16)
    k_rope = kr * cos + krr * sin
    for hd in range(MLA_HEADS):
        sl = slice(hd * HEAD_PAD, (hd + 1) * HEAD_PAD)
        q_ref[0, :, sl] = ((q[:, sl] * cos + qr[:, sl] * sin) * q_scale).astype(BF16)
        k_ref[0, :, sl] = (k[:, sl] + k_rope).astype(BF16)


def _mixer_in(x, g, win, gq, wuq, wuqr, gkv, wuk, wuvt, ones, cos, sin, *, q_rank, kv_rank, f_w, q_scale):
    b, s, d = x.shape
    tm = _tile(s, KV_CHUNK)
    hw = MLA_HEADS * HEAD_PAD
    tok = lambda w: pl.BlockSpec((1, tm, w), lambda bi, i: (bi, i, 0))
    rope = pl.BlockSpec((tm, HEAD_PAD), lambda bi, i: (i, 0))
    kern = functools.partial(_mixer_in_kernel, q_rank=q_rank, kv_rank=kv_rank, f_w=f_w, q_scale=q_scale)
    return pl.pallas_call(
        kern,
        grid=(b, s // tm),
        in_specs=[tok(d), _const_spec(g.shape), _const_spec(win.shape), _const_spec(gq.shape),
                  _const_spec(wuq.shape), _const_spec(wuqr.shape), _const_spec(gkv.shape),
                  _const_spec(wuk.shape), _const_spec(wuvt.shape), _const_spec(ones.shape), rope, rope],
        out_specs=[tok(hw), tok(hw),
                   pl.BlockSpec((1, MLA_HEADS, 1, VT_ROWS, tm), lambda bi, i: (bi, 0, i, 0, 0)),
                   tok(f_w)],
        out_shape=[jax.ShapeDtypeStruct((b, s, hw), BF16), jax.ShapeDtypeStruct((b, s, hw), BF16),
                   jax.ShapeDtypeStruct((b, MLA_HEADS, s // tm, VT_ROWS, tm), BF16),
                   jax.ShapeDtypeStruct((b, s, f_w), F32)],
        compiler_params=_params("parallel", "parallel"),
        name="mixer_in",
    )(x, g, win, gq, wuq, wuqr, gkv, wuk, wuvt, ones, cos, sin)


def _flash_kernel(q_ref, k_ref, vt_ref, o_ref, s_ref):
    tq = q_ref.shape[1]
    nk, _, tk = vt_ref.shape[2:]
    q = q_ref[0]

    def scores(j, slot):
        k = k_ref[0, pl.ds(pl.multiple_of(j * tk, tk), tk), :]
        s = lax.dot_general(k, q, (((1,), (1,)), ((), ())), preferred_element_type=F32)
        s_ref[slot] = s
        return jnp.max(s, axis=0, keepdims=True)

    def update(j, slot, m, acc, cmax):
        m_new = jnp.maximum(m, cmax)
        alpha = jnp.exp2(m - m_new)
        p = jnp.exp2(s_ref[slot] - m_new).astype(BF16)
        return m_new, alpha * acc + _dot(vt_ref[0, 0, j], p)

    m = jnp.full((1, tq), -jnp.inf, F32)
    acc = jnp.zeros((VT_ROWS, tq), F32)
    if nk >= SLOTS:
        assert nk % SLOTS == 0, nk
        c0 = scores(0, 0)
        c1 = scores(1, 1)

        def body(t, carry):
            m, acc, ca, cb = carry
            base = SLOTS * t
            for u in range(SLOTS):
                cn = scores(base + u + 2, (u + 2) % SLOTS)
                m, acc = update(base + u, u, m, acc, ca)
                ca, cb = cb, cn
            return m, acc, ca, cb

        m, acc, ca, cb = lax.fori_loop(0, nk // SLOTS - 1, body, (m, acc, c0, c1))
        base = nk - SLOTS
        for u in range(SLOTS):
            cn = scores(base + u + 2, (u + 2) % SLOTS) if u + 2 < SLOTS else None
            m, acc = update(base + u, u, m, acc, ca)
            ca, cb = cb, cn
    else:
        for j in range(nk):
            m, acc = update(j, j, m, acc, scores(j, j))
    o_ref[0] = (acc[:V_HEAD] / acc[V_HEAD:V_HEAD + 1]).astype(BF16)


def _flash(q, k, vt):
    b, s, _ = q.shape
    tq = _tile(s, 1024)
    _, _, nk, _, tk = vt.shape
    return pl.pallas_call(
        _flash_kernel,
        grid=(b, MLA_HEADS, s // tq),
        in_specs=[pl.BlockSpec((1, tq, HEAD_PAD), lambda bi, h, i: (bi, i, h)),
                  pl.BlockSpec((1, s, HEAD_PAD), lambda bi, h, i: (bi, 0, h)),
                  pl.BlockSpec((1, 1, nk, VT_ROWS, tk), lambda bi, h, i: (bi, h, 0, 0, 0))],
        out_specs=pl.BlockSpec((1, V_HEAD, tq), lambda bi, h, i: (bi, h, i)),
        out_shape=jax.ShapeDtypeStruct((b, MLA_HEADS * V_HEAD, s), BF16),
        scratch_shapes=[pltpu.VMEM((SLOTS, tk, tq), F32)],
        compiler_params=_params("parallel", "parallel", "arbitrary"),
        name="mla_flash",
    )(q, k, vt)


def _split(x):
    hi = x.astype(BF16)
    return hi, (x - hi.astype(F32)).astype(BF16)


def _split_stack(w, axis):
    hi, lo = _split(w)
    return jnp.concatenate([hi, lo], axis=axis)


def _const_times_data(wcat, x):
    m = wcat.shape[0] // 2
    x_hi, x_lo = _split(x)
    r = _dot(wcat, x_hi)
    return r[:m] + r[m:] + _dot(wcat[:m], x_lo)


def _dft_a_kernel(x_ref, w_ref, ar_ref, ai_ref):
    n1 = x_ref.shape[1]
    r = _const_times_data(w_ref[...], x_ref[0])
    ar_ref[0] = r[:n1]
    ai_ref[0] = r[n1:]


def _dft_a(x2, wcat):
    b, n1, w = x2.shape
    tc = _tile(w, 4096)
    blk = pl.BlockSpec((1, n1, tc), lambda bi, j: (bi, 0, j))
    sds = jax.ShapeDtypeStruct((b, n1, w), F32)
    return pl.pallas_call(
        _dft_a_kernel,
        grid=(b, w // tc),
        in_specs=[blk, _const_spec(wcat.shape)],
        out_specs=[blk, blk],
        out_shape=[sds, sds],
        compiler_params=_params("parallel", "parallel"),
        name="fnet_dft_a",
    )(x2, wcat)


def _dft_b_kernel(ar_ref, ai_ref, g_ref, wc_ref, o_ref, *, kb, scale):
    n2, w = ar_ref.shape[2:]
    wc = wc_ref[...]
    for j in range(kb):
        a = jnp.concatenate([ar_ref[0, j], ai_ref[0, j]], axis=1)
        p = _const_times_data(g_ref[j], a)
        yr = p[:n2, :w] - p[n2:, w:]
        yi = p[:n2, w:] + p[n2:, :w]
        lhs = jnp.concatenate(
            [jnp.concatenate([yr[:, g * FNET_CH:(g + 1) * FNET_CH], yi[:, g * FNET_CH:(g + 1) * FNET_CH]], axis=1)
             for g in range(FNET_GROUPS)], axis=0)
        l_hi, l_lo = _split(lhs)
        r = _dot(l_hi, wc)
        res = (r[:, :FNET_CH] + r[:, FNET_CH:] + _dot(l_lo, wc[:, :FNET_CH])) * scale
        for g in range(FNET_GROUPS):
            o_ref[0, :, j, g * FNET_CH:(g + 1) * FNET_CH] = res[g * n2:(g + 1) * n2]


def _dft_b(ar, ai, gcat, wc, scale):
    b, n1, n2, w = ar.shape
    kb = 8
    assert n1 % kb == 0
    a_blk = pl.BlockSpec((1, kb, n2, w), lambda bi, i: (bi, i, 0, 0))
    g_blk = pl.BlockSpec((kb,) + gcat.shape[1:], lambda bi, i: (i, 0, 0))
    return pl.pallas_call(
        functools.partial(_dft_b_kernel, kb=kb, scale=scale),
        grid=(b, n1 // kb),
        in_specs=[a_blk, a_blk, g_blk, _const_spec(wc.shape)],
        out_specs=pl.BlockSpec((1, n2, kb, w), lambda bi, i: (bi, 0, i, 0)),
        out_shape=jax.ShapeDtypeStruct((b, n2, n1, w), F32),
        compiler_params=_params("parallel", "parallel"),
        name="fnet_dft_b",
    )(ar, ai, gcat, wc)


def _dft_tables(s):
    n1 = 1 << ((s.bit_length() - 1 + 1) // 2)
    n2 = s // n1
    assert n1 * n2 == s and n1 % 8 == 0, (s, n1, n2)

    def cs(idx, period):
        ang = (2.0 * math.pi / period) * (idx % period).astype(F32)
        return jnp.cos(ang), jnp.sin(ang)

    i1 = jnp.arange(n1, dtype=jnp.int32)
    i2 = jnp.arange(n2, dtype=jnp.int32)
    c1, s1 = cs(i1[:, None] * i1[None, :], n1)
    wa = _split_stack(jnp.concatenate([c1, -s1], axis=0), 0)
    idx = (i2[None, None, :] * (i1[:, None, None] + n1 * i2[None, :, None])) % s
    gr, gi_pos = cs(idx, s)
    gcat = _split_stack(jnp.concatenate([gr, -gi_pos], axis=1), 1)
    ic = jnp.arange(FNET_CH, dtype=jnp.int32)
    cc, sc = cs(ic[:, None] * ic[None, :], FNET_CH)
    wc = _split_stack(jnp.concatenate([cc, sc], axis=0), 1)
    return n1, n2, wa, gcat, wc


def _fnet(f):
    b, s, w = f.shape
    n1, n2, wa, gcat, wc = _dft_tables(s)
    ar, ai = _dft_a(f.reshape(b, n1, n2 * w), wa)
    out = _dft_b(ar.reshape(b, n1, n2, w), ai.reshape(b, n1, n2, w), gcat, wc,
                 1.0 / math.sqrt(s * FNET_CH))
    return out.reshape(b, s, w)


def _mixer_out_kernel(x_ref, at_ref, f_ref, wa_ref, wf_ref, o_ref):
    attn = lax.dot_general(at_ref[0], wa_ref[...], (((0,), (0,)), ((), ())), preferred_element_type=F32)
    o_ref[0] = x_ref[0] + attn + _dot(f_ref[0].astype(BF16), wf_ref[...])


def _mixer_out(x, attn_t, fo, wa, wf):
    b, s, d = x.shape
    tm = _tile(s, 512)
    tok = lambda w: pl.BlockSpec((1, tm, w), lambda bi, i: (bi, i, 0))
    return pl.pallas_call(
        _mixer_out_kernel,
        grid=(b, s // tm),
        in_specs=[tok(d), pl.BlockSpec((1, attn_t.shape[1], tm), lambda bi, i: (bi, 0, i)),
                  tok(fo.shape[-1]), _const_spec(wa.shape), _const_spec(wf.shape)],
        out_specs=tok(d),
        out_shape=jax.ShapeDtypeStruct(x.shape, F32),
        compiler_params=_params("parallel", "parallel"),
        name="mixer_out",
    )(x, attn_t, fo, wa, wf)


def _conv_in_kernel(x_ref, g_ref, w_ref, b_ref, u_ref):
    d = x_ref.shape[-1]
    h = _rms(x_ref[0], g_ref[...]).astype(BF16)
    z = _dot(h, w_ref[...]) + b_ref[...]
    u_ref[0] = z[:, :d] * jax.nn.sigmoid(z[:, d:])


def _conv_in(x, g, w, bias):
    b, s, d = x.shape
    tm = _tile(s, 512)
    tok = pl.BlockSpec((1, tm, d), lambda bi, i: (bi, i, 0))
    return pl.pallas_call(
        _conv_in_kernel,
        grid=(b, s // tm),
        in_specs=[tok, _const_spec(g.shape), _const_spec(w.shape), _const_spec(bias.shape)],
        out_specs=tok,
        out_shape=jax.ShapeDtypeStruct(x.shape, F32),
        compiler_params=_params("parallel", "parallel"),
        name="conv_in",
    )(x, g, w, bias)


HALO = 16


def _conv_out_kernel(x_ref, u_ref, ul_ref, ur_ref, wdw_ref, bdw_ref, gln_ref, bln_ref, w2_ref, b2_ref,
                     o_ref, buf_ref, c_ref, *, taps, rows):
    i = pl.program_id(1)
    n = pl.num_programs(1)
    tm = u_ref.shape[1]
    d = u_ref.shape[2]
    half = taps // 2
    nlb = d // LANES
    for cb in range(nlb):
        ls = slice(cb * LANES, (cb + 1) * LANES)
        buf_ref[cb, 0:HALO] = jnp.where(i > 0, ul_ref[0, :, ls], 0.0)
        buf_ref[cb, HALO:HALO + tm] = u_ref[0, :, ls]
        buf_ref[cb, HALO + tm:HALO + tm + HALO] = jnp.where(i < n - 1, ur_ref[0, :, ls], 0.0)

    def lane_block(cb, _):
        for r0 in range(0, tm, rows):
            acc = jnp.zeros((rows, LANES), F32)
            for t in range(taps):
                lo = r0 + HALO - half + t
                acc = acc + buf_ref[cb, lo:lo + rows] * wdw_ref[cb, t:t + 1]
            c_ref[cb, r0:r0 + rows] = acc
        return 0

    lax.fori_loop(0, nlb, lane_block, 0)
    c = jnp.concatenate([c_ref[cb] for cb in range(nlb)], axis=1) + bdw_ref[...]
    mu = jnp.mean(c, axis=-1, keepdims=True)
    cc = c - mu
    y = cc * lax.rsqrt(jnp.mean(cc * cc, axis=-1, keepdims=True) + EPS)
    y = y * gln_ref[...] + bln_ref[...]
    y = y * jax.nn.sigmoid(y)
    o_ref[0] = x_ref[0] + _dot(y.astype(BF16), w2_ref[...]) + b2_ref[...]


def _conv_out(x, u, wdw, bdw, gln, bln, w2, b2):
    b, s, d = x.shape
    taps = wdw.shape[0]
    assert taps // 2 <= HALO
    tm = _tile(s, 256)
    hb = tm // HALO
    nh = s // HALO
    tok = pl.BlockSpec((1, tm, d), lambda bi, i: (bi, i, 0))
    left = pl.BlockSpec((1, HALO, d), lambda bi, i: (bi, jnp.maximum(i * hb - 1, 0), 0))
    right = pl.BlockSpec((1, HALO, d), lambda bi, i: (bi, jnp.minimum((i + 1) * hb, nh - 1), 0))
    kern = functools.partial(_conv_out_kernel, taps=taps, rows=32)
    nlb = d // LANES
    wdw = wdw.reshape(taps, nlb, LANES).transpose(1, 0, 2)
    return pl.pallas_call(
        kern,
        grid=(b, s // tm),
        in_specs=[tok, tok, left, right, _const_spec(wdw.shape), _const_spec(bdw.shape),
                  _const_spec(gln.shape), _const_spec(bln.shape), _const_spec(w2.shape), _const_spec(b2.shape)],
        out_specs=tok,
        out_shape=jax.ShapeDtypeStruct(x.shape, F32),
        scratch_shapes=[pltpu.VMEM((nlb, tm + 2 * HALO, LANES), F32), pltpu.VMEM((nlb, tm, LANES), F32)],
        compiler_params=_params("parallel", "parallel"),
        name="conv_out",
    )(x, u, u, u, wdw, bdw, gln, bln, w2, b2)


def _mem_kv_kernel(m_ref, g_ref, w_ref, o_ref):
    h = _rms(m_ref[0], g_ref[...]).astype(BF16)
    o_ref[0] = _dot(h, w_ref[...]).astype(BF16)


def _mem_kv(mem, g, wkv):
    b, nm, d = mem.shape
    n = wkv.shape[1]
    return pl.pallas_call(
        _mem_kv_kernel,
        grid=(b,),
        in_specs=[pl.BlockSpec((1, nm, d), lambda bi: (bi, 0, 0)), _const_spec(g.shape), _const_spec(wkv.shape)],
        out_specs=pl.BlockSpec((1, nm, n), lambda bi: (bi, 0, 0)),
        out_shape=jax.ShapeDtypeStruct((b, nm, n), BF16),
        compiler_params=_params("parallel"),
        name="mem_kv",
    )(mem, g, wkv)


def _xattn_kernel(x_ref, g_ref, wq_ref, kv_ref, wo_ref, o_ref, *, scale):
    x = x_ref[0]
    d = x.shape[-1]
    hd = d // XA_HEADS
    h = _rms(x, g_ref[...]).astype(BF16)
    q = (_dot(h, wq_ref[...]) * scale).astype(BF16)
    outs = []
    for a in range(XA_HEADS):
        k = kv_ref[0, :, a * hd:(a + 1) * hd]
        v = kv_ref[0, :, d + a * hd:d + (a + 1) * hd]
        s = lax.dot_general(q[:, a * hd:(a + 1) * hd], k, (((1,), (1,)), ((), ())),
                            preferred_element_type=F32)
        p = jnp.exp(s - jnp.max(s, axis=1, keepdims=True))
        l = jnp.sum(p, axis=1, keepdims=True)
        outs.append((_dot(p.astype(BF16), v) / l).astype(BF16))
    o = jnp.concatenate(outs, axis=1)
    o_ref[0] = x + _dot(o, wo_ref[...])


def _xattn(x, g, wq, kv, wo):
    b, s, d = x.shape
    tm = _tile(s, 512)
    tok = pl.BlockSpec((1, tm, d), lambda bi, i: (bi, i, 0))
    return pl.pallas_call(
        functools.partial(_xattn_kernel, scale=(d // XA_HEADS) ** -0.5),
        grid=(b, s // tm),
        in_specs=[tok, _const_spec(g.shape), _const_spec(wq.shape),
                  pl.BlockSpec((1,) + kv.shape[1:], lambda bi, i: (bi, 0, 0)), _const_spec(wo.shape)],
        out_specs=tok,
        out_shape=jax.ShapeDtypeStruct(x.shape, F32),
        compiler_params=_params("parallel", "parallel"),
        name="mem_xattn",
    )(x, g, wq, kv, wo)


def _ffn_kernel(x_ref, g_ref, w1_ref, w3_ref, w2_ref, gf_ref, o_ref, *, final_norm):
    x = x_ref[0]
    h = _rms(x, g_ref[...]).astype(BF16)
    acc = x
    for c0 in range(0, w1_ref.shape[1], FF_CHUNK):
        a = _dot(h, w1_ref[:, c0:c0 + FF_CHUNK])
        bb = _dot(h, w3_ref[:, c0:c0 + FF_CHUNK])
        acc = acc + _dot((a * jax.nn.sigmoid(a) * bb).astype(BF16), w2_ref[c0:c0 + FF_CHUNK, :])
    if final_norm:
        acc = _rms(acc, gf_ref[...])
    o_ref[0] = acc


def _ffn(x, g, w1, w3, w2, gf, final_norm):
    b, s, d = x.shape
    tm = _tile(s, 512)
    tok = pl.BlockSpec((1, tm, d), lambda bi, i: (bi, i, 0))
    assert w1.shape[1] % FF_CHUNK == 0
    wspec = lambda w: pl.BlockSpec(w.shape, lambda bi, i: (0, 0), pipeline_mode=pl.Buffered(1))
    return pl.pallas_call(
        functools.partial(_ffn_kernel, final_norm=final_norm),
        grid=(b, s // tm),
        in_specs=[tok, _const_spec(g.shape), wspec(w1), wspec(w3), wspec(w2), _const_spec(gf.shape)],
        out_specs=tok,
        out_shape=jax.ShapeDtypeStruct(x.shape, F32),
        compiler_params=_params("parallel", "parallel"),
        name="swiglu_ffn",
    )(x, g, w1, w3, w2, gf)


def _rope_tables(s):
    half = QK_ROPE // 2
    inv = ROPE_THETA ** (-jnp.arange(half, dtype=F32) / half)
    ang = jnp.arange(s, dtype=F32)[:, None] * inv[None, :]
    cos = jnp.cos(ang)
    sin = jnp.sin(ang)
    ones = jnp.ones((s, QK_NOPE), F32)
    tail = HEAD_PAD - QK_NOPE - QK_ROPE
    cos_t = jnp.concatenate([ones, cos, cos, jnp.ones((s, tail), F32)], axis=1)
    sin_t = jnp.concatenate([0 * ones, -sin, sin, jnp.zeros((s, tail), F32)], axis=1)
    return cos_t, sin_t


def _swap_halves(w):
    half = QK_ROPE // 2
    return jnp.concatenate([w[..., half:], w[..., :half]], axis=-1)


def _prep_mixer(w_in, w_uq, w_ukv, w_out):
    d = w_in.shape[0]
    q_rank = w_uq.shape[0]
    kv_rank = w_ukv.shape[0]
    f_w = FNET_GROUPS * FNET_CH
    tail = HEAD_PAD - QK_NOPE - QK_ROPE
    o = q_rank + kv_rank
    w_kr = w_in[:, o:o + QK_ROPE]
    pad_rope = lambda w: jnp.concatenate(
        [jnp.zeros(w.shape[:-1] + (QK_NOPE,), F32), w, jnp.zeros(w.shape[:-1] + (tail,), F32)], axis=-1)
    win = jnp.concatenate([w_in[:, :o], w_in[:, o + QK_ROPE:], pad_rope(w_kr), pad_rope(_swap_halves(w_kr))],
                          axis=1).astype(BF16)
    uq = w_uq.reshape(q_rank, MLA_HEADS, QK_NOPE + QK_ROPE)
    zt = jnp.zeros((q_rank, MLA_HEADS, tail), F32)
    wuq = jnp.concatenate([uq, zt], axis=-1).reshape(q_rank, -1).astype(BF16)
    wuqr = jnp.concatenate([jnp.zeros((q_rank, MLA_HEADS, QK_NOPE), F32), _swap_halves(uq[..., QK_NOPE:]), zt],
                           axis=-1).reshape(q_rank, -1).astype(BF16)
    ukv = w_ukv.reshape(kv_rank, MLA_HEADS, QK_NOPE + V_HEAD)
    wuk = jnp.concatenate([ukv[..., :QK_NOPE], jnp.zeros((kv_rank, MLA_HEADS, HEAD_PAD - QK_NOPE), F32)],
                          axis=-1).reshape(kv_rank, -1).astype(BF16)
    wuvt = jnp.concatenate([ukv[..., QK_NOPE:], jnp.zeros((kv_rank, MLA_HEADS, VT_ROWS - V_HEAD), F32)],
                           axis=-1).reshape(kv_rank, -1).T.astype(BF16)
    ones = (jnp.arange(MLA_HEADS * VT_ROWS) % VT_ROWS == V_HEAD).astype(F32).reshape(-1, 1)
    av = MLA_HEADS * V_HEAD
    return dict(win=win, wuq=wuq, wuqr=wuqr, wuk=wuk, wuvt=wuvt, ones=ones,
                wa=w_out[:av].astype(BF16), wf=w_out[av:].astype(BF16),
                q_rank=q_rank, kv_rank=kv_rank, f_w=f_w)


def _row(v):
    return v.reshape(1, -1).astype(F32)


def _trunk(x, mem, p):
    depth = p['g_mix'].shape[0]
    s = x.shape[1]
    for i in range(depth):
        j = i // 2
        if i % 2 == 0:
            mp = p['mixer'][j]
            cos_t, sin_t = _rope_tables(s)
            q_scale = (QK_NOPE + QK_ROPE) ** -0.5 * math.log2(math.e)
            q, k, vt, f = _mixer_in(x, _row(p['g_mix'][i]), mp['win'], _row(p['mla_g_q'][j]), mp['wuq'], mp['wuqr'],
                                    _row(p['mla_g_kv'][j]), mp['wuk'], mp['wuvt'], mp['ones'], cos_t, sin_t,
                                    q_rank=mp['q_rank'], kv_rank=mp['kv_rank'], f_w=mp['f_w'], q_scale=q_scale)
            attn_t = _flash(q, k, vt)
            fo = _fnet(f)
            x = _mixer_out(x, attn_t, fo, mp['wa'], mp['wf'])
        else:
            u = _conv_in(x, _row(p['g_mix'][i]), p['conv_w_pw1'][j], _row(p['conv_b_pw1'][j]))
            x = _conv_out(x, u, p['conv_w_dw'][j], _row(p['conv_b_dw'][j]), _row(p['conv_g_ln'][j]),
                          _row(p['conv_b_ln'][j]), p['conv_w_pw2'][j], _row(p['conv_b_pw2'][j]))
        kv = _mem_kv(mem, _row(p['g_xkv'][i]), p['xa_wkv'][i])
        x = _xattn(x, _row(p['g_xq'][i]), p['xa_wq'][i], kv, p['xa_wo'][i])
        x = _ffn(x, _row(p['g_ffn'][i]), p['ffn_w1'][i], p['ffn_w3'][i], p['ffn_w2'][i],
                 _row(p['g_final']), final_norm=(i == depth - 1))
    return x


def kernel(x_prompt, x_sample, mem_prompt, mem_sample, g_mix, g_xq, g_xkv, xa_wq, xa_wk, xa_wv, xa_wo, g_ffn, ffn_w1, ffn_w3, ffn_w2, ab_w_in, mla_g_q, mla_w_uq, mla_g_kv, mla_w_ukv, ab_w_out, conv_w_pw1, conv_b_pw1, conv_w_dw, conv_b_dw, conv_g_ln, conv_b_ln, conv_w_pw2, conv_b_pw2, g_final):
    p = {
        'g_mix': g_mix, 'g_xq': g_xq, 'g_xkv': g_xkv, 'g_ffn': g_ffn, 'g_final': g_final,
        'xa_wq': xa_wq.astype(BF16), 'xa_wo': xa_wo.astype(BF16),
        'xa_wkv': jnp.concatenate([xa_wk, xa_wv], axis=-1).astype(BF16),
        'ffn_w1': ffn_w1.astype(BF16), 'ffn_w3': ffn_w3.astype(BF16), 'ffn_w2': ffn_w2.astype(BF16),
        'mla_g_q': mla_g_q, 'mla_g_kv': mla_g_kv,
        'mixer': [_prep_mixer(ab_w_in[j], mla_w_uq[j], mla_w_ukv[j], ab_w_out[j])
                  for j in range(ab_w_in.shape[0])],
        'conv_w_pw1': conv_w_pw1.astype(BF16), 'conv_b_pw1': conv_b_pw1,
        'conv_w_dw': conv_w_dw, 'conv_b_dw': conv_b_dw, 'conv_g_ln': conv_g_ln, 'conv_b_ln': conv_b_ln,
        'conv_w_pw2': conv_w_pw2.astype(BF16), 'conv_b_pw2': conv_b_pw2,
    }
    return (_trunk(x_prompt, mem_prompt, p), _trunk(x_sample, mem_sample, p))
```

```python
import functools
import math

import jax
import jax.numpy as jnp
from jax import lax
from jax.experimental import pallas as pl
from jax.experimental.pallas import tpu as pltpu

MLA_HEADS = 8
QK_NOPE = 64
QK_ROPE = 32
V_HEAD = 64
ROPE_THETA = 10000.0
FNET_GROUPS = 4
FNET_CH = 128
XA_HEADS = 4
EPS = 1e-6

LANES = 128
HEAD_PAD = 128
VT_ROWS = 80
KV_CHUNK = 512
SLOTS = 8
FLAT_MAX_STEPS = 32
FLAT_SLOTS = 4
FF_CHUNK = 256
VMEM_LIMIT = 52 * 1024 * 1024

F32 = jnp.float32
BF16 = jnp.bfloat16


def _params(*sem):
    return pltpu.CompilerParams(dimension_semantics=sem, vmem_limit_bytes=VMEM_LIMIT)


def _rms(x, g):
    ms = jnp.mean(x * x, axis=-1, keepdims=True)
    return x * lax.rsqrt(ms + EPS) * g


def _dot(a, b):
    return jnp.dot(a, b, preferred_element_type=F32)


def _const_spec(shape):
    nd = len(shape)
    return pl.BlockSpec(shape, lambda *_: (0,) * nd)


def _tile(n, want):
    t = min(n, want)
    assert n % t == 0, (n, t)
    return t


def _mixer_in_kernel(x_ref, g_ref, win_ref, gq_ref, wuq_ref, wuqr_ref, gkv_ref, wuk_ref, wuvt_ref, ones_ref,
                     cos_ref, sin_ref, q_ref, k_ref, vt_ref, f_ref, *, q_rank, kv_rank, f_w, q_scale):
    x = x_ref[0]
    h = _rms(x, g_ref[...]).astype(BF16)
    z = _dot(h, win_ref[...])
    o = 0
    cq = z[:, o:o + q_rank]; o += q_rank
    ckv = z[:, o:o + kv_rank]; o += kv_rank
    f_ref[0] = z[:, o:o + f_w]; o += f_w
    kr = z[:, o:o + HEAD_PAD]; o += HEAD_PAD
    krr = z[:, o:o + HEAD_PAD]
    cos = cos_ref[...]
    sin = sin_ref[...]
    cqn = _rms(cq, gq_ref[...]).astype(BF16)
    q = _dot(cqn, wuq_ref[...])
    qr = _dot(cqn, wuqr_ref[...])
    ckvn = _rms(ckv, gkv_ref[...]).astype(BF16)
    k = _dot(ckvn, wuk_ref[...])
    vt = lax.dot_general(wuvt_ref[...], ckvn, (((1,), (1,)), ((), ())), preferred_element_type=F32)
    vt = vt + ones_ref[...]
    for hd in range(MLA_HEADS):
        vt_ref[0, hd, 0] = vt[hd * VT_ROWS:(hd + 1) * VT_ROWS].astype(BF16)
    k_rope = kr * cos + krr * sin
    for hd in range(MLA_HEADS):
        sl = slice(hd * HEAD_PAD, (hd + 1) * HEAD_PAD)
        q_ref[0, :, sl] = ((q[:, sl] * cos + qr[:, sl] * sin) * q_scale).astype(BF16)
        k_ref[0, :, sl] = (k[:, sl] + k_rope).astype(BF16)


def _mixer_in(x, g, win, gq, wuq, wuqr, gkv, wuk, wuvt, ones, cos, sin, *, q_rank, kv_rank, f_w, q_scale):
    b, s, d = x.shape
    tm = _tile(s, KV_CHUNK)
    hw = MLA_HEADS * HEAD_PAD
    tok = lambda w: pl.BlockSpec((1, tm, w), lambda bi, i: (bi, i, 0))
    rope = pl.BlockSpec((tm, HEAD_PAD), lambda bi, i: (i, 0))
    kern = functools.partial(_mixer_in_kernel, q_rank=q_rank, kv_rank=kv_rank, f_w=f_w, q_scale=q_scale)
    return pl.pallas_call(
        kern,
        grid=(b, s // tm),
        in_specs=[tok(d), _const_spec(g.shape), _const_spec(win.shape), _const_spec(gq.shape),
                  _const_spec(wuq.shape), _const_spec(wuqr.shape), _const_spec(gkv.shape),
                  _const_spec(wuk.shape), _const_spec(wuvt.shape), _const_spec(ones.shape), rope, rope],
        out_specs=[tok(hw), tok(hw),
                   pl.BlockSpec((1, MLA_HEADS, 1, VT_ROWS, tm), lambda bi, i: (bi, 0, i, 0, 0)),
                   tok(f_w)],
        out_shape=[jax.ShapeDtypeStruct((b, s, hw), BF16), jax.ShapeDtypeStruct((b, s, hw), BF16),
                   jax.ShapeDtypeStruct((b, MLA_HEADS, s // tm, VT_ROWS, tm), BF16),
                   jax.ShapeDtypeStruct((b, s, f_w), F32)],
        compiler_params=_params("parallel", "parallel"),
        name="mixer_in",
    )(x, g, win, gq, wuq, wuqr, gkv, wuk, wuvt, ones, cos, sin)


def _q_transposed(q_ref, t, tq):
    return q_ref[0, t * tq:(t + 1) * tq, :].astype(F32).T.astype(BF16)


def _scores(k_ref, s_ref, qt, row0, slot):
    tk = s_ref.shape[1]
    s = _dot(k_ref[0, pl.ds(row0, tk), :], qt)
    s_ref[slot] = s
    return jnp.max(s, axis=0, keepdims=True)


def _softmax_update(vt_ref, s_ref, j, slot, m, acc, cmax):
    m_new = jnp.maximum(m, cmax)
    alpha = jnp.exp2(m - m_new)
    p = jnp.exp2(s_ref[slot] - m_new).astype(BF16)
    return m_new, alpha * acc + _dot(vt_ref[0, 0, j], p)


def _attn_init(tq):
    return jnp.full((1, tq), -jnp.inf, F32), jnp.zeros((VT_ROWS, tq), F32)


def _attn_out(acc):
    return (acc[:V_HEAD] / acc[V_HEAD:V_HEAD + 1]).astype(BF16)


def _flash_loop_kernel(q_ref, k_ref, vt_ref, o_ref, s_ref):
    tq = q_ref.shape[1]
    nk, _, tk = vt_ref.shape[2:]
    qt = _q_transposed(q_ref, 0, tq)
    scores = lambda j, slot: _scores(k_ref, s_ref, qt, pl.multiple_of(j * tk, tk), slot)
    update = functools.partial(_softmax_update, vt_ref, s_ref)
    m, acc = _attn_init(tq)
    c0 = scores(0, 0)
    c1 = scores(1, 1)

    def body(t, carry):
        m, acc, ca, cb = carry
        base = SLOTS * t
        for u in range(SLOTS):
            cn = scores(base + u + 2, (u + 2) % SLOTS)
            m, acc = update(base + u, u, m, acc, ca)
            ca, cb = cb, cn
        return m, acc, ca, cb

    m, acc, ca, cb = lax.fori_loop(0, nk // SLOTS - 1, body, (m, acc, c0, c1))
    base = nk - SLOTS
    for u in range(SLOTS):
        cn = scores(base + u + 2, (u + 2) % SLOTS) if u + 2 < SLOTS else None
        m, acc = update(base + u, u, m, acc, ca)
        ca, cb = cb, cn
    o_ref[0] = _attn_out(acc)


def _flash_flat_kernel(q_ref, k_ref, vt_ref, o_ref, s_ref, *, tq):
    nk, _, tk = vt_ref.shape[2:]
    nslots = s_ref.shape[0]
    steps = [(t, j) for t in range(q_ref.shape[1] // tq) for j in range(nk)]
    qts = {}

    def scores(n):
        t, j = steps[n]
        if t not in qts:
            qts[t] = _q_transposed(q_ref, t, tq)
        return _scores(k_ref, s_ref, qts[t], j * tk, n % nslots)

    cmax = {n: scores(n) for n in range(min(2, len(steps)))}
    m, acc = _attn_init(tq)
    for n, (t, j) in enumerate(steps):
        if n + 2 < len(steps):
            cmax[n + 2] = scores(n + 2)
        m, acc = _softmax_update(vt_ref, s_ref, j, n % nslots, m, acc, cmax.pop(n))
        if j == nk - 1:
            o_ref[0, :, t * tq:(t + 1) * tq] = _attn_out(acc)
            m, acc = _attn_init(tq)


def _flash(q, k, vt):
    b, s, _ = q.shape
    tq = _tile(s, 1024)
    _, _, nk, _, tk = vt.shape
    flat = (s // tq) * nk <= FLAT_MAX_STEPS
    if flat:
        kern, tqb, slots = functools.partial(_flash_flat_kernel, tq=tq), s, FLAT_SLOTS
    else:
        assert nk % SLOTS == 0, nk
        kern, tqb, slots = _flash_loop_kernel, tq, SLOTS
    return pl.pallas_call(
        kern,
        grid=(b, MLA_HEADS, s // tqb),
        in_specs=[pl.BlockSpec((1, tqb, HEAD_PAD), lambda bi, h, i: (bi, i, h)),
                  pl.BlockSpec((1, s, HEAD_PAD), lambda bi, h, i: (bi, 0, h)),
                  pl.BlockSpec((1, 1, nk, VT_ROWS, tk), lambda bi, h, i: (bi, h, 0, 0, 0))],
        out_specs=pl.BlockSpec((1, V_HEAD, tqb), lambda bi, h, i: (bi, h, i)),
        out_shape=jax.ShapeDtypeStruct((b, MLA_HEADS * V_HEAD, s), BF16),
        scratch_shapes=[pltpu.VMEM((slots, tk, tq), F32)],
        compiler_params=_params("parallel", "parallel", "arbitrary"),
        name="mla_flash",
    )(q, k, vt)


def _split(x):
    hi = lax.bitcast_convert_type(lax.bitcast_convert_type(x, jnp.uint32) & jnp.uint32(0xFFFF0000), F32)
    return hi.astype(BF16), (x - hi).astype(BF16)


def _split_stack(w, axis):
    hi, lo = _split(w)
    return jnp.concatenate([hi, lo], axis=axis)


def _const_times_data(wcat, x):
    m = wcat.shape[0] // 2
    x_hi, x_lo = _split(x)
    r = _dot(wcat, x_hi)
    return r[:m] + r[m:] + _dot(wcat[:m], x_lo)


def _dft_a_kernel(x_ref, w_ref, ar_ref, ai_ref):
    n1 = x_ref.shape[1]
    r = _const_times_data(w_ref[...], x_ref[0])
    ar_ref[0] = r[:n1]
    ai_ref[0] = r[n1:]


def _dft_a(x2, wcat):
    b, n1, w = x2.shape
    tc = _tile(w, 4096)
    blk = pl.BlockSpec((1, n1, tc), lambda bi, j: (bi, 0, j))
    sds = jax.ShapeDtypeStruct((b, n1, w), F32)
    return pl.pallas_call(
        _dft_a_kernel,
        grid=(b, w // tc),
        in_specs=[blk, _const_spec(wcat.shape)],
        out_specs=[blk, blk],
        out_shape=[sds, sds],
        compiler_params=_params("parallel", "parallel"),
        name="fnet_dft_a",
    )(x2, wcat)


def _dft_b_kernel(ar_ref, ai_ref, g_ref, wc_ref, o_ref, *, kb, scale):
    n2, w = ar_ref.shape[2:]
    wc = wc_ref[...]
    for j in range(kb):
        a = jnp.concatenate([ar_ref[0, j], ai_ref[0, j]], axis=1)
        p = _const_times_data(g_ref[j], a)
        yr = p[:n2, :w] - p[n2:, w:]
        yi = p[:n2, w:] + p[n2:, :w]
        lhs = jnp.concatenate(
            [jnp.concatenate([yr[:, g * FNET_CH:(g + 1) * FNET_CH], yi[:, g * FNET_CH:(g + 1) * FNET_CH]], axis=1)
             for g in range(FNET_GROUPS)], axis=0)
        l_hi, l_lo = _split(lhs)
        r = _dot(l_hi, wc)
        res = (r[:, :FNET_CH] + r[:, FNET_CH:] + _dot(l_lo, wc[:, :FNET_CH])) * scale
        for g in range(FNET_GROUPS):
            o_ref[0, :, j, g * FNET_CH:(g + 1) * FNET_CH] = res[g * n2:(g + 1) * n2]


def _dft_b(ar, ai, gcat, wc, scale):
    b, n1, n2, w = ar.shape
    kb = 8
    assert n1 % kb == 0
    a_blk = pl.BlockSpec((1, kb, n2, w), lambda bi, i: (bi, i, 0, 0))
    g_blk = pl.BlockSpec((kb,) + gcat.shape[1:], lambda bi, i: (i, 0, 0))
    return pl.pallas_call(
        functools.partial(_dft_b_kernel, kb=kb, scale=scale),
        grid=(b, n1 // kb),
        in_specs=[a_blk, a_blk, g_blk, _const_spec(wc.shape)],
        out_specs=pl.BlockSpec((1, n2, kb, w), lambda bi, i: (bi, 0, i, 0)),
        out_shape=jax.ShapeDtypeStruct((b, n2, n1, w), F32),
        compiler_params=_params("parallel", "parallel"),
        name="fnet_dft_b",
    )(ar, ai, gcat, wc)


def _dft_tables(s):
    n1 = 1 << ((s.bit_length() - 1 + 1) // 2)
    n2 = s // n1
    assert n1 * n2 == s and n1 % 8 == 0, (s, n1, n2)

    def cs(idx, period):
        ang = (2.0 * math.pi / period) * (idx % period).astype(F32)
        return jnp.cos(ang), jnp.sin(ang)

    i1 = jnp.arange(n1, dtype=jnp.int32)
    i2 = jnp.arange(n2, dtype=jnp.int32)
    c1, s1 = cs(i1[:, None] * i1[None, :], n1)
    wa = _split_stack(jnp.concatenate([c1, -s1], axis=0), 0)
    c2, s2 = cs(i2[:, None] * i2[None, :], n2)
    ct, st = cs(i1[:, None] * i2[None, :], s)
    gr = c2[None] * ct[:, None, :] - s2[None] * st[:, None, :]
    gi = -(s2[None] * ct[:, None, :] + c2[None] * st[:, None, :])
    gcat = _split_stack(jnp.concatenate([gr, gi], axis=1), 1)
    ic = jnp.arange(FNET_CH, dtype=jnp.int32)
    cc, sc = cs(ic[:, None] * ic[None, :], FNET_CH)
    wc = _split_stack(jnp.concatenate([cc, sc], axis=0), 1)
    return n1, n2, wa, gcat, wc


def _fnet(f):
    b, s, w = f.shape
    n1, n2, wa, gcat, wc = _dft_tables(s)
    ar, ai = _dft_a(f.reshape(b, n1, n2 * w), wa)
    out = _dft_b(ar.reshape(b, n1, n2, w), ai.reshape(b, n1, n2, w), gcat, wc,
                 1.0 / math.sqrt(s * FNET_CH))
    return out.reshape(b, s, w)


def _mixer_out_kernel(x_ref, at_ref, f_ref, wa_ref, wf_ref, o_ref):
    attn = lax.dot_general(at_ref[0], wa_ref[...], (((0,), (0,)), ((), ())), preferred_element_type=F32)
    o_ref[0] = x_ref[0] + attn + _dot(f_ref[0].astype(BF16), wf_ref[...])


def _mixer_out(x, attn_t, fo, wa, wf):
    b, s, d = x.shape
    tm = _tile(s, 512)
    tok = lambda w: pl.BlockSpec((1, tm, w), lambda bi, i: (bi, i, 0))
    return pl.pallas_call(
        _mixer_out_kernel,
        grid=(b, s // tm),
        in_specs=[tok(d), pl.BlockSpec((1, attn_t.shape[1], tm), lambda bi, i: (bi, 0, i)),
                  tok(fo.shape[-1]), _const_spec(wa.shape), _const_spec(wf.shape)],
        out_specs=tok(d),
        out_shape=jax.ShapeDtypeStruct(x.shape, F32),
        compiler_params=_params("parallel", "parallel"),
        name="mixer_out",
    )(x, attn_t, fo, wa, wf)


def _conv_in_kernel(x_ref, g_ref, w_ref, b_ref, u_ref):
    d = x_ref.shape[-1]
    h = _rms(x_ref[0], g_ref[...]).astype(BF16)
    z = _dot(h, w_ref[...]) + b_ref[...]
    u_ref[0] = z[:, :d] * jax.nn.sigmoid(z[:, d:])


def _conv_in(x, g, w, bias):
    b, s, d = x.shape
    tm = _tile(s, 512)
    tok = pl.BlockSpec((1, tm, d), lambda bi, i: (bi, i, 0))
    return pl.pallas_call(
        _conv_in_kernel,
        grid=(b, s // tm),
        in_specs=[tok, _const_spec(g.shape), _const_spec(w.shape), _const_spec(bias.shape)],
        out_specs=tok,
        out_shape=jax.ShapeDtypeStruct(x.shape, F32),
        compiler_params=_params("parallel", "parallel"),
        name="conv_in",
    )(x, g, w, bias)


HALO = 16


def _conv_out_kernel(x_ref, u_ref, ul_ref, ur_ref, wdw_ref, bdw_ref, gln_ref, bln_ref, w2_ref, b2_ref,
                     o_ref, buf_ref, c_ref, *, taps, rows):
    i = pl.program_id(1)
    n = pl.num_programs(1)
    tm = u_ref.shape[1]
    d = u_ref.shape[2]
    half = taps // 2
    nlb = d // LANES
    for cb in range(nlb):
        ls = slice(cb * LANES, (cb + 1) * LANES)
        buf_ref[cb, 0:HALO] = jnp.where(i > 0, ul_ref[0, :, ls], 0.0)
        buf_ref[cb, HALO:HALO + tm] = u_ref[0, :, ls]
        buf_ref[cb, HALO + tm:HALO + tm + HALO] = jnp.where(i < n - 1, ur_ref[0, :, ls], 0.0)

    def lane_block(cb, _):
        for r0 in range(0, tm, rows):
            acc = jnp.zeros((rows, LANES), F32)
            for t in range(taps):
                lo = r0 + HALO - half + t
                acc = acc + buf_ref[cb, lo:lo + rows] * wdw_ref[cb, t:t + 1]
            c_ref[cb, r0:r0 + rows] = acc
        return 0

    lax.fori_loop(0, nlb, lane_block, 0)
    c = jnp.concatenate([c_ref[cb] for cb in range(nlb)], axis=1) + bdw_ref[...]
    mu = jnp.mean(c, axis=-1, keepdims=True)
    cc = c - mu
    y = cc * lax.rsqrt(jnp.mean(cc * cc, axis=-1, keepdims=True) + EPS)
    y = y * gln_ref[...] + bln_ref[...]
    y = y * jax.nn.sigmoid(y)
    o_ref[0] = x_ref[0] + _dot(y.astype(BF16), w2_ref[...]) + b2_ref[...]


def _conv_out(x, u, wdw, bdw, gln, bln, w2, b2):
    b, s, d = x.shape
    taps = wdw.shape[0]
    assert taps // 2 <= HALO
    tm = _tile(s, 256)
    hb = tm // HALO
    nh = s // HALO
    tok = pl.BlockSpec((1, tm, d), lambda bi, i: (bi, i, 0))
    left = pl.BlockSpec((1, HALO, d), lambda bi, i: (bi, jnp.maximum(i * hb - 1, 0), 0))
    right = pl.BlockSpec((1, HALO, d), lambda bi, i: (bi, jnp.minimum((i + 1) * hb, nh - 1), 0))
    kern = functools.partial(_conv_out_kernel, taps=taps, rows=32)
    nlb = d // LANES
    wdw = wdw.reshape(taps, nlb, LANES).transpose(1, 0, 2)
    return pl.pallas_call(
        kern,
        grid=(b, s // tm),
        in_specs=[tok, tok, left, right, _const_spec(wdw.shape), _const_spec(bdw.shape),
                  _const_spec(gln.shape), _const_spec(bln.shape), _const_spec(w2.shape), _const_spec(b2.shape)],
        out_specs=tok,
        out_shape=jax.ShapeDtypeStruct(x.shape, F32),
        scratch_shapes=[pltpu.VMEM((nlb, tm + 2 * HALO, LANES), F32), pltpu.VMEM((nlb, tm, LANES), F32)],
        compiler_params=_params("parallel", "parallel"),
        name="conv_out",
    )(x, u, u, u, wdw, bdw, gln, bln, w2, b2)


def _mem_kv_kernel(m_ref, g_ref, w_ref, o_ref):
    h = _rms(m_ref[0], g_ref[...]).astype(BF16)
    o_ref[0] = _dot(h, w_ref[...]).astype(BF16)


def _mem_kv(mem, g, wkv):
    b, nm, d = mem.shape
    n = wkv.shape[1]
    return pl.pallas_call(
        _mem_kv_kernel,
        grid=(b,),
        in_specs=[pl.BlockSpec((1, nm, d), lambda bi: (bi, 0, 0)), _const_spec(g.shape), _const_spec(wkv.shape)],
        out_specs=pl.BlockSpec((1, nm, n), lambda bi: (bi, 0, 0)),
        out_shape=jax.ShapeDtypeStruct((b, nm, n), BF16),
        compiler_params=_params("parallel"),
        name="mem_kv",
    )(mem, g, wkv)


def _xattn_kernel(x_ref, g_ref, wq_ref, kv_ref, wo_ref, o_ref, *, scale):
    x = x_ref[0]
    d = x.shape[-1]
    hd = d // XA_HEADS
    h = _rms(x, g_ref[...]).astype(BF16)
    q = (_dot(h, wq_ref[...]) * scale).astype(BF16)
    outs = []
    for a in range(XA_HEADS):
        k = kv_ref[0, :, a * hd:(a + 1) * hd]
        v = kv_ref[0, :, d + a * hd:d + (a + 1) * hd]
        s = lax.dot_general(q[:, a * hd:(a + 1) * hd], k, (((1,), (1,)), ((), ())),
                            preferred_element_type=F32)
        p = jnp.exp(s - jnp.max(s, axis=1, keepdims=True))
        l = jnp.sum(p, axis=1, keepdims=True)
        outs.append((_dot(p.astype(BF16), v) / l).astype(BF16))
    o = jnp.concatenate(outs, axis=1)
    o_ref[0] = x + _dot(o, wo_ref[...])


def _xattn(x, g, wq, kv, wo):
    b, s, d = x.shape
    tm = _tile(s, 512)
    tok = pl.BlockSpec((1, tm, d), lambda bi, i: (bi, i, 0))
    return pl.pallas_call(
        functools.partial(_xattn_kernel, scale=(d // XA_HEADS) ** -0.5),
        grid=(b, s // tm),
        in_specs=[tok, _const_spec(g.shape), _const_spec(wq.shape),
                  pl.BlockSpec((1,) + kv.shape[1:], lambda bi, i: (bi, 0, 0)), _const_spec(wo.shape)],
        out_specs=tok,
        out_shape=jax.ShapeDtypeStruct(x.shape, F32),
        compiler_params=_params("parallel", "parallel"),
        name="mem_xattn",
    )(x, g, wq, kv, wo)


def _ffn_kernel(x_ref, g_ref, w1_ref, w3_ref, w2_ref, gf_ref, o_ref, *, final_norm):
    x = x_ref[0]
    h = _rms(x, g_ref[...]).astype(BF16)
    acc = x
    for c0 in range(0, w1_ref.shape[1], FF_CHUNK):
        a = _dot(h, w1_ref[:, c0:c0 + FF_CHUNK])
        bb = _dot(h, w3_ref[:, c0:c0 + FF_CHUNK])
        acc = acc + _dot((a * jax.nn.sigmoid(a) * bb).astype(BF16), w2_ref[c0:c0 + FF_CHUNK, :])
    if final_norm:
        acc = _rms(acc, gf_ref[...])
    o_ref[0] = acc


def _ffn(x, g, w1, w3, w2, gf, final_norm):
    b, s, d = x.shape
    tm = _tile(s, 512)
    tok = pl.BlockSpec((1, tm, d), lambda bi, i: (bi, i, 0))
    assert w1.shape[1] % FF_CHUNK == 0
    wspec = lambda w: pl.BlockSpec(w.shape, lambda bi, i: (0, 0), pipeline_mode=pl.Buffered(1))
    return pl.pallas_call(
        functools.partial(_ffn_kernel, final_norm=final_norm),
        grid=(b, s // tm),
        in_specs=[tok, _const_spec(g.shape), wspec(w1), wspec(w3), wspec(w2), _const_spec(gf.shape)],
        out_specs=tok,
        out_shape=jax.ShapeDtypeStruct(x.shape, F32),
        compiler_params=_params("parallel", "parallel"),
        name="swiglu_ffn",
    )(x, g, w1, w3, w2, gf)


def _rope_tables(s):
    half = QK_ROPE // 2
    inv = ROPE_THETA ** (-jnp.arange(half, dtype=F32) / half)
    ang = jnp.arange(s, dtype=F32)[:, None] * inv[None, :]
    cos = jnp.cos(ang)
    sin = jnp.sin(ang)
    ones = jnp.ones((s, QK_NOPE), F32)
    tail = HEAD_PAD - QK_NOPE - QK_ROPE
    cos_t = jnp.concatenate([ones, cos, cos, jnp.ones((s, tail), F32)], axis=1)
    sin_t = jnp.concatenate([0 * ones, -sin, sin, jnp.zeros((s, tail), F32)], axis=1)
    return cos_t, sin_t


def _swap_halves(w):
    half = QK_ROPE // 2
    return jnp.concatenate([w[..., half:], w[..., :half]], axis=-1)


def _prep_mixer(w_in, w_uq, w_ukv, w_out):
    d = w_in.shape[0]
    q_rank = w_uq.shape[0]
    kv_rank = w_ukv.shape[0]
    f_w = FNET_GROUPS * FNET_CH
    tail = HEAD_PAD - QK_NOPE - QK_ROPE
    o = q_rank + kv_rank
    w_kr = w_in[:, o:o + QK_ROPE]
    pad_rope = lambda w: jnp.concatenate(
        [jnp.zeros(w.shape[:-1] + (QK_NOPE,), F32), w, jnp.zeros(w.shape[:-1] + (tail,), F32)], axis=-1)
    win = jnp.concatenate([w_in[:, :o], w_in[:, o + QK_ROPE:], pad_rope(w_kr), pad_rope(_swap_halves(w_kr))],
                          axis=1).astype(BF16)
    uq = w_uq.reshape(q_rank, MLA_HEADS, QK_NOPE + QK_ROPE)
    zt = jnp.zeros((q_rank, MLA_HEADS, tail), F32)
    wuq = jnp.concatenate([uq, zt], axis=-1).reshape(q_rank, -1).astype(BF16)
    wuqr = jnp.concatenate([jnp.zeros((q_rank, MLA_HEADS, QK_NOPE), F32), _swap_halves(uq[..., QK_NOPE:]), zt],
                           axis=-1).reshape(q_rank, -1).astype(BF16)
    ukv = w_ukv.reshape(kv_rank, MLA_HEADS, QK_NOPE + V_HEAD)
    wuk = jnp.concatenate([ukv[..., :QK_NOPE], jnp.zeros((kv_rank, MLA_HEADS, HEAD_PAD - QK_NOPE), F32)],
                          axis=-1).reshape(kv_rank, -1).astype(BF16)
    wuvt = jnp.concatenate([ukv[..., QK_NOPE:], jnp.zeros((kv_rank, MLA_HEADS, VT_ROWS - V_HEAD), F32)],
                           axis=-1).reshape(kv_rank, -1).T.astype(BF16)
    ones = (jnp.arange(MLA_HEADS * VT_ROWS) % VT_ROWS == V_HEAD).astype(F32).reshape(-1, 1)
    av = MLA_HEADS * V_HEAD
    return dict(win=win, wuq=wuq, wuqr=wuqr, wuk=wuk, wuvt=wuvt, ones=ones,
                wa=w_out[:av].astype(BF16), wf=w_out[av:].astype(BF16),
                q_rank=q_rank, kv_rank=kv_rank, f_w=f_w)


def _row(v):
    return v.reshape(1, -1).astype(F32)


def _trunk(x, mem, p):
    depth = p['g_mix'].shape[0]
    s = x.shape[1]
    for i in range(depth):
        j = i // 2
        if i % 2 == 0:
            mp = p['mixer'][j]
            cos_t, sin_t = _rope_tables(s)
            q_scale = (QK_NOPE + QK_ROPE) ** -0.5 * math.log2(math.e)
            q, k, vt, f = _mixer_in(x, _row(p['g_mix'][i]), mp['win'], _row(p['mla_g_q'][j]), mp['wuq'], mp['wuqr'],
                                    _row(p['mla_g_kv'][j]), mp['wuk'], mp['wuvt'], mp['ones'], cos_t, sin_t,
                                    q_rank=mp['q_rank'], kv_rank=mp['kv_rank'], f_w=mp['f_w'], q_scale=q_scale)
            attn_t = _flash(q, k, vt)
            fo = _fnet(f)
            x = _mixer_out(x, attn_t, fo, mp['wa'], mp['wf'])
        else:
            u = _conv_in(x, _row(p['g_mix'][i]), p['conv_w_pw1'][j], _row(p['conv_b_pw1'][j]))
            x = _conv_out(x, u, p['conv_w_dw'][j], _row(p['conv_b_dw'][j]), _row(p['conv_g_ln'][j]),
                          _row(p['conv_b_ln'][j]), p['conv_w_pw2'][j], _row(p['conv_b_pw2'][j]))
        kv = _mem_kv(mem, _row(p['g_xkv'][i]), p['xa_wkv'][i])
        x = _xattn(x, _row(p['g_xq'][i]), p['xa_wq'][i], kv, p['xa_wo'][i])
        x = _ffn(x, _row(p['g_ffn'][i]), p['ffn_w1'][i], p['ffn_w3'][i], p['ffn_w2'][i],
                 _row(p['g_final']), final_norm=(i == depth - 1))
    return x


def kernel(x_prompt, x_sample, mem_prompt, mem_sample, g_mix, g_xq, g_xkv, xa_wq, xa_wk, xa_wv, xa_wo, g_ffn, ffn_w1, ffn_w3, ffn_w2, ab_w_in, mla_g_q, mla_w_uq, mla_g_kv, mla_w_ukv, ab_w_out, conv_w_pw1, conv_b_pw1, conv_w_dw, conv_b_dw, conv_g_ln, conv_b_ln, conv_w_pw2, conv_b_pw2, g_final):
    p = {
        'g_mix': g_mix, 'g_xq': g_xq, 'g_xkv': g_xkv, 'g_ffn': g_ffn, 'g_final': g_final,
        'xa_wq': xa_wq.astype(BF16), 'xa_wo': xa_wo.astype(BF16),
        'xa_wkv': jnp.concatenate([xa_wk, xa_wv], axis=-1).astype(BF16),
        'ffn_w1': ffn_w1.astype(BF16), 'ffn_w3': ffn_w3.astype(BF16), 'ffn_w2': ffn_w2.astype(BF16),
        'mla_g_q': mla_g_q, 'mla_g_kv': mla_g_kv,
        'mixer': [_prep_mixer(ab_w_in[j], mla_w_uq[j], mla_w_ukv[j], ab_w_out[j])
                  for j in range(ab_w_in.shape[0])],
        'conv_w_pw1': conv_w_pw1.astype(BF16), 'conv_b_pw1': conv_b_pw1,
        'conv_w_dw': conv_w_dw, 'conv_b_dw': conv_b_dw, 'conv_g_ln': conv_g_ln, 'conv_b_ln': conv_b_ln,
        'conv_w_pw2': conv_w_pw2.astype(BF16), 'conv_b_pw2': conv_b_pw2,
    }
    return (_trunk(x_prompt, mem_prompt, p), _trunk(x_sample, mem_sample, p))
```

```python
import functools
import math

import jax
import jax.numpy as jnp
from jax import lax
from jax.experimental import pallas as pl
from jax.experimental.pallas import tpu as pltpu

MLA_HEADS = 8
QK_NOPE = 64
QK_ROPE = 32
V_HEAD = 64
ROPE_THETA = 10000.0
FNET_GROUPS = 4
FNET_CH = 128
XA_HEADS = 4
EPS = 1e-6

LANES = 128
HEAD_PAD = 128
VT_ROWS = 80
KV_CHUNK = 512
SLOTS = 8
TILE_SLOTS = 4
FF_CHUNK = 256
VMEM_LIMIT = 52 * 1024 * 1024

F32 = jnp.float32
BF16 = jnp.bfloat16


def _params(*sem):
    return pltpu.CompilerParams(dimension_semantics=sem, vmem_limit_bytes=VMEM_LIMIT)


def _rms(x, g):
    ms = jnp.mean(x * x, axis=-1, keepdims=True)
    return x * lax.rsqrt(ms + EPS) * g


def _dot(a, b):
    return jnp.dot(a, b, preferred_element_type=F32)


def _const_spec(shape):
    nd = len(shape)
    return pl.BlockSpec(shape, lambda *_: (0,) * nd)


def _tile(n, want):
    t = min(n, want)
    assert n % t == 0, (n, t)
    return t


def _mixer_in_kernel(x_ref, g_ref, win_ref, gq_ref, wuq_ref, wuqr_ref, gkv_ref, wuk_ref, wuvt_ref, ones_ref,
                     cos_ref, sin_ref, q_ref, k_ref, vt_ref, f_ref, *, q_rank, kv_rank, f_w, q_scale):
    x = x_ref[0]
    h = _rms(x, g_ref[...]).astype(BF16)
    z = _dot(h, win_ref[...])
    o = 0
    cq = z[:, o:o + q_rank]; o += q_rank
    ckv = z[:, o:o + kv_rank]; o += kv_rank
    f_ref[0] = z[:, o:o + f_w]; o += f_w
    kr = z[:, o:o + HEAD_PAD]; o += HEAD_PAD
    krr = z[:, o:o + HEAD_PAD]
    cos = cos_ref[...]
    sin = sin_ref[...]
    cqn = _rms(cq, gq_ref[...]).astype(BF16)
    q = _dot(cqn, wuq_ref[...])
    qr = _dot(cqn, wuqr_ref[...])
    ckvn = _rms(ckv, gkv_ref[...]).astype(BF16)
    k = _dot(ckvn, wuk_ref[...])
    vt = lax.dot_general(wuvt_ref[...], ckvn, (((1,), (1,)), ((), ())), preferred_element_type=F32)
    vt = vt + ones_ref[...]
    tk = vt_ref.shape[-1]
    for hd in range(MLA_HEADS):
        for c in range(vt_ref.shape[2]):
            vt_ref[0, hd, c] = vt[hd * VT_ROWS:(hd + 1) * VT_ROWS, c * tk:(c + 1) * tk].astype(BF16)
    k_rope = kr * cos + krr * sin
    for hd in range(MLA_HEADS):
        sl = slice(hd * HEAD_PAD, (hd + 1) * HEAD_PAD)
        q_ref[0, :, sl] = ((q[:, sl] * cos + qr[:, sl] * sin) * q_scale).astype(BF16)
        k_ref[0, :, sl] = (k[:, sl] + k_rope).astype(BF16)


def _mixer_in(x, g, win, gq, wuq, wuqr, gkv, wuk, wuvt, ones, cos, sin, *, q_rank, kv_rank, f_w, q_scale):
    b, s, d = x.shape
    tm = _tile(s, 512)
    tk = _tile(tm, KV_CHUNK)
    cpt = tm // tk
    hw = MLA_HEADS * HEAD_PAD
    tok = lambda w: pl.BlockSpec((1, tm, w), lambda bi, i: (bi, i, 0))
    rope = pl.BlockSpec((tm, HEAD_PAD), lambda bi, i: (i, 0))
    kern = functools.partial(_mixer_in_kernel, q_rank=q_rank, kv_rank=kv_rank, f_w=f_w, q_scale=q_scale)
    return pl.pallas_call(
        kern,
        grid=(b, s // tm),
        in_specs=[tok(d), _const_spec(g.shape), _const_spec(win.shape), _const_spec(gq.shape),
                  _const_spec(wuq.shape), _const_spec(wuqr.shape), _const_spec(gkv.shape),
                  _const_spec(wuk.shape), _const_spec(wuvt.shape), _const_spec(ones.shape), rope, rope],
        out_specs=[tok(hw), tok(hw),
                   pl.BlockSpec((1, MLA_HEADS, cpt, VT_ROWS, tk), lambda bi, i: (bi, 0, i, 0, 0)),
                   tok(f_w)],
        out_shape=[jax.ShapeDtypeStruct((b, s, hw), BF16), jax.ShapeDtypeStruct((b, s, hw), BF16),
                   jax.ShapeDtypeStruct((b, MLA_HEADS, s // tk, VT_ROWS, tk), BF16),
                   jax.ShapeDtypeStruct((b, s, f_w), F32)],
        compiler_params=_params("parallel", "parallel"),
        name="mixer_in",
    )(x, g, win, gq, wuq, wuqr, gkv, wuk, wuvt, ones, cos, sin)


def _q_transposed(q_ref, t, tq):
    return q_ref[0, t * tq:(t + 1) * tq, :].astype(F32).T.astype(BF16)


def _scores(k_ref, s_ref, qt, row0, slot):
    tk = s_ref.shape[1]
    s = _dot(k_ref[0, pl.ds(row0, tk), :], qt)
    s_ref[slot] = s
    return jnp.max(s, axis=0, keepdims=True)


def _softmax_update(vt_ref, s_ref, j, slot, m, acc, cmax):
    m_new = jnp.maximum(m, cmax)
    alpha = jnp.exp2(m - m_new)
    p = jnp.exp2(s_ref[slot] - m_new).astype(BF16)
    return m_new, alpha * acc + _dot(vt_ref[0, 0, j], p)


def _attn_init(tq):
    return jnp.full((1, tq), -jnp.inf, F32), jnp.zeros((VT_ROWS, tq), F32)


def _attn_out(acc):
    return (acc[:V_HEAD] / acc[V_HEAD:V_HEAD + 1]).astype(BF16)


def _flash_loop_kernel(q_ref, k_ref, vt_ref, o_ref, s_ref):
    tq = q_ref.shape[1]
    nk, _, tk = vt_ref.shape[2:]
    qt = _q_transposed(q_ref, 0, tq)
    scores = lambda j, slot: _scores(k_ref, s_ref, qt, pl.multiple_of(j * tk, tk), slot)
    update = functools.partial(_softmax_update, vt_ref, s_ref)
    m, acc = _attn_init(tq)
    c0 = scores(0, 0)
    c1 = scores(1, 1)

    def body(t, carry):
        m, acc, ca, cb = carry
        base = SLOTS * t
        for u in range(SLOTS):
            cn = scores(base + u + 2, (u + 2) % SLOTS)
            m, acc = update(base + u, u, m, acc, ca)
            ca, cb = cb, cn
        return m, acc, ca, cb

    m, acc, ca, cb = lax.fori_loop(0, nk // SLOTS - 1, body, (m, acc, c0, c1))
    base = nk - SLOTS
    for u in range(SLOTS):
        cn = scores(base + u + 2, (u + 2) % SLOTS) if u + 2 < SLOTS else None
        m, acc = update(base + u, u, m, acc, ca)
        ca, cb = cb, cn
    o_ref[0] = _attn_out(acc)


def _flash_tiles_kernel(q_ref, k_ref, vt_ref, o_ref, s_ref, *, tq):
    nk, _, tk = vt_ref.shape[2:]
    nt = q_ref.shape[1] // tq
    nslots = s_ref.shape[0]
    assert nslots >= 3 and nk % nslots == 0, (nk, nslots)

    def tile(t, qt, qt_next, ca, cb):
        col0 = t * tq if isinstance(t, int) else pl.multiple_of(t * tq, tq)
        m, acc = _attn_init(tq)
        for u in range(nk):
            if u + 2 < nk:
                cn = _scores(k_ref, s_ref, qt, (u + 2) * tk, (u + 2) % nslots)
            elif qt_next is not None:
                cn = _scores(k_ref, s_ref, qt_next, (u + 2 - nk) * tk, (u + 2 - nk) % nslots)
            else:
                cn = None
            m, acc = _softmax_update(vt_ref, s_ref, u, u % nslots, m, acc, ca)
            ca, cb = cb, cn
        o_ref[0, :, pl.ds(col0, tq)] = _attn_out(acc)
        return ca, cb

    qt = _q_transposed(q_ref, 0, tq)
    c0 = _scores(k_ref, s_ref, qt, 0, 0)
    c1 = _scores(k_ref, s_ref, qt, tk, 1)

    def body(t, carry):
        qt, ca, cb = carry
        qt_next = q_ref[0, pl.ds(pl.multiple_of((t + 1) * tq, tq), tq), :].astype(F32).T.astype(BF16)
        ca, cb = tile(t, qt, qt_next, ca, cb)
        return qt_next, ca, cb

    qt, ca, cb = lax.fori_loop(0, nt - 1, body, (qt, c0, c1))
    tile(nt - 1, qt, None, ca, cb)


def _flash(q, k, vt):
    b, s, _ = q.shape
    tq = _tile(s, 1024)
    _, _, nk, _, tk = vt.shape
    if nk <= SLOTS:
        kern, tqb, slots = functools.partial(_flash_tiles_kernel, tq=tq), s, TILE_SLOTS
    else:
        assert nk % SLOTS == 0, nk
        kern, tqb, slots = _flash_loop_kernel, tq, SLOTS
    return pl.pallas_call(
        kern,
        grid=(b, MLA_HEADS, s // tqb),
        in_specs=[pl.BlockSpec((1, tqb, HEAD_PAD), lambda bi, h, i: (bi, i, h)),
                  pl.BlockSpec((1, s, HEAD_PAD), lambda bi, h, i: (bi, 0, h)),
                  pl.BlockSpec((1, 1, nk, VT_ROWS, tk), lambda bi, h, i: (bi, h, 0, 0, 0))],
        out_specs=pl.BlockSpec((1, V_HEAD, tqb), lambda bi, h, i: (bi, h, i)),
        out_shape=jax.ShapeDtypeStruct((b, MLA_HEADS * V_HEAD, s), BF16),
        scratch_shapes=[pltpu.VMEM((slots, tk, tq), F32)],
        compiler_params=_params("parallel", "parallel", "arbitrary"),
        name="mla_flash",
    )(q, k, vt)


def _split(x):
    hi = lax.bitcast_convert_type(lax.bitcast_convert_type(x, jnp.uint32) & jnp.uint32(0xFFFF0000), F32)
    return hi.astype(BF16), (x - hi).astype(BF16)


def _split_stack(w, axis):
    hi, lo = _split(w)
    return jnp.concatenate([hi, lo], axis=axis)


def _const_times_data(wcat, x):
    m = wcat.shape[0] // 2
    x_hi, x_lo = _split(x)
    r = _dot(wcat, x_hi)
    return r[:m] + r[m:] + _dot(wcat[:m], x_lo)


def _dft_a_kernel(x_ref, w_ref, ar_ref, ai_ref):
    n1 = x_ref.shape[1]
    r = _const_times_data(w_ref[...], x_ref[0])
    ar_ref[0] = r[:n1]
    ai_ref[0] = r[n1:]


def _dft_a(x2, wcat):
    b, n1, w = x2.shape
    tc = _tile(w, 4096)
    blk = pl.BlockSpec((1, n1, tc), lambda bi, j: (bi, 0, j))
    sds = jax.ShapeDtypeStruct((b, n1, w), F32)
    return pl.pallas_call(
        _dft_a_kernel,
        grid=(b, w // tc),
        in_specs=[blk, _const_spec(wcat.shape)],
        out_specs=[blk, blk],
        out_shape=[sds, sds],
        compiler_params=_params("parallel", "parallel"),
        name="fnet_dft_a",
    )(x2, wcat)


def _dft_b_kernel(ar_ref, ai_ref, g_ref, wc_ref, o_ref, *, kb, scale):
    n2, w = ar_ref.shape[2:]
    wc = wc_ref[...]
    for j in range(kb):
        a = jnp.concatenate([ar_ref[0, j], ai_ref[0, j]], axis=1)
        p = _const_times_data(g_ref[j], a)
        yr = p[:n2, :w] - p[n2:, w:]
        yi = p[:n2, w:] + p[n2:, :w]
        lhs = jnp.concatenate(
            [jnp.concatenate([yr[:, g * FNET_CH:(g + 1) * FNET_CH], yi[:, g * FNET_CH:(g + 1) * FNET_CH]], axis=1)
             for g in range(FNET_GROUPS)], axis=0)
        l_hi, l_lo = _split(lhs)
        r = _dot(l_hi, wc)
        res = (r[:, :FNET_CH] + r[:, FNET_CH:] + _dot(l_lo, wc[:, :FNET_CH])) * scale
        for g in range(FNET_GROUPS):
            o_ref[0, :, j, g * FNET_CH:(g + 1) * FNET_CH] = res[g * n2:(g + 1) * n2]


def _dft_b(ar, ai, gcat, wc, scale):
    b, n1, n2, w = ar.shape
    kb = 8
    assert n1 % kb == 0
    a_blk = pl.BlockSpec((1, kb, n2, w), lambda bi, i: (bi, i, 0, 0))
    g_blk = pl.BlockSpec((kb,) + gcat.shape[1:], lambda bi, i: (i, 0, 0))
    return pl.pallas_call(
        functools.partial(_dft_b_kernel, kb=kb, scale=scale),
        grid=(b, n1 // kb),
        in_specs=[a_blk, a_blk, g_blk, _const_spec(wc.shape)],
        out_specs=pl.BlockSpec((1, n2, kb, w), lambda bi, i: (bi, 0, i, 0)),
        out_shape=jax.ShapeDtypeStruct((b, n2, n1, w), F32),
        compiler_params=_params("parallel", "parallel"),
        name="fnet_dft_b",
    )(ar, ai, gcat, wc)


def _dft_tables(s):
    n1 = 1 << ((s.bit_length() - 1 + 1) // 2)
    n2 = s // n1
    assert n1 * n2 == s and n1 % 8 == 0, (s, n1, n2)

    def cs(idx, period):
        ang = (2.0 * math.pi / period) * (idx % period).astype(F32)
        return jnp.cos(ang), jnp.sin(ang)

    i1 = jnp.arange(n1, dtype=jnp.int32)
    i2 = jnp.arange(n2, dtype=jnp.int32)
    c1, s1 = cs(i1[:, None] * i1[None, :], n1)
    wa = _split_stack(jnp.concatenate([c1, -s1], axis=0), 0)
    c2, s2 = cs(i2[:, None] * i2[None, :], n2)
    ct, st = cs(i1[:, None] * i2[None, :], s)
    c2, s2, ct, st = lax.optimization_barrier((c2, s2, ct, st))
    gr =c2[None] * ct[:, None, :] - s2[None] * st[:, None, :]
    gi = -(s2[None] * ct[:, None, :] + c2[None] * st[:, None, :])
    gcat = _split_stack(jnp.concatenate([gr, gi], axis=1), 1)
    ic = jnp.arange(FNET_CH, dtype=jnp.int32)
    cc, sc = cs(ic[:, None] * ic[None, :], FNET_CH)
    wc = _split_stack(jnp.concatenate([cc, sc], axis=0), 1)
    return n1, n2, wa, gcat, wc


def _fnet(f):
    b, s, w = f.shape
    n1, n2, wa, gcat, wc = _dft_tables(s)
    ar, ai = _dft_a(f.reshape(b, n1, n2 * w), wa)
    out = _dft_b(ar.reshape(b, n1, n2, w), ai.reshape(b, n1, n2, w), gcat, wc,
                 1.0 / math.sqrt(s * FNET_CH))
    return out.reshape(b, s, w)


def _mixer_out_kernel(x_ref, at_ref, f_ref, wa_ref, wf_ref, o_ref):
    attn = lax.dot_general(at_ref[0], wa_ref[...], (((0,), (0,)), ((), ())), preferred_element_type=F32)
    o_ref[0] = x_ref[0] + attn + _dot(f_ref[0].astype(BF16), wf_ref[...])


def _mixer_out(x, attn_t, fo, wa, wf):
    b, s, d = x.shape
    tm = _tile(s, 512)
    tok = lambda w: pl.BlockSpec((1, tm, w), lambda bi, i: (bi, i, 0))
    return pl.pallas_call(
        _mixer_out_kernel,
        grid=(b, s // tm),
        in_specs=[tok(d), pl.BlockSpec((1, attn_t.shape[1], tm), lambda bi, i: (bi, 0, i)),
                  tok(fo.shape[-1]), _const_spec(wa.shape), _const_spec(wf.shape)],
        out_specs=tok(d),
        out_shape=jax.ShapeDtypeStruct(x.shape, F32),
        compiler_params=_params("parallel", "parallel"),
        name="mixer_out",
    )(x, attn_t, fo, wa, wf)


def _conv_in_kernel(x_ref, g_ref, w_ref, b_ref, u_ref):
    d = x_ref.shape[-1]
    h = _rms(x_ref[0], g_ref[...]).astype(BF16)
    z = _dot(h, w_ref[...]) + b_ref[...]
    u_ref[0] = z[:, :d] * jax.nn.sigmoid(z[:, d:])


def _conv_in(x, g, w, bias):
    b, s, d = x.shape
    tm = _tile(s, 512)
    tok = pl.BlockSpec((1, tm, d), lambda bi, i: (bi, i, 0))
    return pl.pallas_call(
        _conv_in_kernel,
        grid=(b, s // tm),
        in_specs=[tok, _const_spec(g.shape), _const_spec(w.shape), _const_spec(bias.shape)],
        out_specs=tok,
        out_shape=jax.ShapeDtypeStruct(x.shape, F32),
        compiler_params=_params("parallel", "parallel"),
        name="conv_in",
    )(x, g, w, bias)


HALO = 16


def _conv_out_kernel(x_ref, u_ref, ul_ref, ur_ref, wdw_ref, bdw_ref, gln_ref, bln_ref, w2_ref, b2_ref,
                     o_ref, buf_ref, c_ref, *, taps, rows):
    i = pl.program_id(1)
    n = pl.num_programs(1)
    tm = u_ref.shape[1]
    d = u_ref.shape[2]
    half = taps // 2
    nlb = d // LANES
    for cb in range(nlb):
        ls = slice(cb * LANES, (cb + 1) * LANES)
        buf_ref[cb, 0:HALO] = jnp.where(i > 0, ul_ref[0, :, ls], 0.0)
        buf_ref[cb, HALO:HALO + tm] = u_ref[0, :, ls]
        buf_ref[cb, HALO + tm:HALO + tm + HALO] = jnp.where(i < n - 1, ur_ref[0, :, ls], 0.0)

    def lane_block(cb, _):
        for r0 in range(0, tm, rows):
            acc = jnp.zeros((rows, LANES), F32)
            for t in range(taps):
                lo = r0 + HALO - half + t
                acc = acc + buf_ref[cb, lo:lo + rows] * wdw_ref[cb, t:t + 1]
            c_ref[cb, r0:r0 + rows] = acc
        return 0

    lax.fori_loop(0, nlb, lane_block, 0)
    c = jnp.concatenate([c_ref[cb] for cb in range(nlb)], axis=1) + bdw_ref[...]
    mu = jnp.mean(c, axis=-1, keepdims=True)
    cc = c - mu
    y = cc * lax.rsqrt(jnp.mean(cc * cc, axis=-1, keepdims=True) + EPS)
    y = y * gln_ref[...] + bln_ref[...]
    y = y * jax.nn.sigmoid(y)
    o_ref[0] = x_ref[0] + _dot(y.astype(BF16), w2_ref[...]) + b2_ref[...]


def _conv_out(x, u, wdw, bdw, gln, bln, w2, b2):
    b, s, d = x.shape
    taps = wdw.shape[0]
    assert taps // 2 <= HALO
    tm = _tile(s, 256)
    hb = tm // HALO
    nh = s // HALO
    tok = pl.BlockSpec((1, tm, d), lambda bi, i: (bi, i, 0))
    left = pl.BlockSpec((1, HALO, d), lambda bi, i: (bi, jnp.maximum(i * hb - 1, 0), 0))
    right = pl.BlockSpec((1, HALO, d), lambda bi, i: (bi, jnp.minimum((i + 1) * hb, nh - 1), 0))
    kern = functools.partial(_conv_out_kernel, taps=taps, rows=32)
    nlb = d // LANES
    wdw = wdw.reshape(taps, nlb, LANES).transpose(1, 0, 2)
    return pl.pallas_call(
        kern,
        grid=(b, s // tm),
        in_specs=[tok, tok, left, right, _const_spec(wdw.shape), _const_spec(bdw.shape),
                  _const_spec(gln.shape), _const_spec(bln.shape), _const_spec(w2.shape), _const_spec(b2.shape)],
        out_specs=tok,
        out_shape=jax.ShapeDtypeStruct(x.shape, F32),
        scratch_shapes=[pltpu.VMEM((nlb, tm + 2 * HALO, LANES), F32), pltpu.VMEM((nlb, tm, LANES), F32)],
        compiler_params=_params("parallel", "parallel"),
        name="conv_out",
    )(x, u, u, u, wdw, bdw, gln, bln, w2, b2)


def _mem_kv_kernel(m_ref, g_ref, w_ref, o_ref):
    h = _rms(m_ref[0], g_ref[...]).astype(BF16)
    o_ref[0] = _dot(h, w_ref[...]).astype(BF16)


def _mem_kv(mem, g, wkv):
    b, nm, d = mem.shape
    n = wkv.shape[1]
    return pl.pallas_call(
        _mem_kv_kernel,
        grid=(b,),
        in_specs=[pl.BlockSpec((1, nm, d), lambda bi: (bi, 0, 0)), _const_spec(g.shape), _const_spec(wkv.shape)],
        out_specs=pl.BlockSpec((1, nm, n), lambda bi: (bi, 0, 0)),
        out_shape=jax.ShapeDtypeStruct((b, nm, n), BF16),
        compiler_params=_params("parallel"),
        name="mem_kv",
    )(mem, g, wkv)


def _xattn_kernel(x_ref, g_ref, wq_ref, kv_ref, wo_ref, o_ref, *, scale):
    x = x_ref[0]
    d = x.shape[-1]
    hd = d // XA_HEADS
    h = _rms(x, g_ref[...]).astype(BF16)
    q = (_dot(h, wq_ref[...]) * scale).astype(BF16)
    outs = []
    for a in range(XA_HEADS):
        k = kv_ref[0, :, a * hd:(a + 1) * hd]
        v = kv_ref[0, :, d + a * hd:d + (a + 1) * hd]
        s = lax.dot_general(q[:, a * hd:(a + 1) * hd], k, (((1,), (1,)), ((), ())),
                            preferred_element_type=F32)
        p = jnp.exp(s - jnp.max(s, axis=1, keepdims=True))
        l = jnp.sum(p, axis=1, keepdims=True)
        outs.append((_dot(p.astype(BF16), v) / l).astype(BF16))
    o = jnp.concatenate(outs, axis=1)
    o_ref[0] = x + _dot(o, wo_ref[...])


def _xattn(x, g, wq, kv, wo):
    b, s, d = x.shape
    tm = _tile(s, 512)
    tok = pl.BlockSpec((1, tm, d), lambda bi, i: (bi, i, 0))
    return pl.pallas_call(
        functools.partial(_xattn_kernel, scale=(d // XA_HEADS) ** -0.5),
        grid=(b, s // tm),
        in_specs=[tok, _const_spec(g.shape), _const_spec(wq.shape),
                  pl.BlockSpec((1,) + kv.shape[1:], lambda bi, i: (bi, 0, 0)), _const_spec(wo.shape)],
        out_specs=tok,
        out_shape=jax.ShapeDtypeStruct(x.shape, F32),
        compiler_params=_params("parallel", "parallel"),
        name="mem_xattn",
    )(x, g, wq, kv, wo)


def _ffn_kernel(x_ref, g_ref, w1_ref, w3_ref, w2_ref, gf_ref, o_ref, *, final_norm):
    x = x_ref[0]
    h = _rms(x, g_ref[...]).astype(BF16)
    acc = x
    for c0 in range(0, w1_ref.shape[1], FF_CHUNK):
        a = _dot(h, w1_ref[:, c0:c0 + FF_CHUNK])
        bb = _dot(h, w3_ref[:, c0:c0 + FF_CHUNK])
        acc = acc + _dot((a * jax.nn.sigmoid(a) * bb).astype(BF16), w2_ref[c0:c0 + FF_CHUNK, :])
    if final_norm:
        acc = _rms(acc, gf_ref[...])
    o_ref[0] = acc


def _ffn(x, g, w1, w3, w2, gf, final_norm):
    b, s, d = x.shape
    tm = _tile(s, 512)
    tok = pl.BlockSpec((1, tm, d), lambda bi, i: (bi, i, 0))
    assert w1.shape[1] % FF_CHUNK == 0
    wspec = lambda w: pl.BlockSpec(w.shape, lambda bi, i: (0, 0), pipeline_mode=pl.Buffered(1))
    return pl.pallas_call(
        functools.partial(_ffn_kernel, final_norm=final_norm),
        grid=(b, s // tm),
        in_specs=[tok, _const_spec(g.shape), wspec(w1), wspec(w3), wspec(w2), _const_spec(gf.shape)],
        out_specs=tok,
        out_shape=jax.ShapeDtypeStruct(x.shape, F32),
        compiler_params=_params("parallel", "parallel"),
        name="swiglu_ffn",
    )(x, g, w1, w3, w2, gf)


def _rope_tables(s):
    half = QK_ROPE // 2
    inv = ROPE_THETA ** (-jnp.arange(half, dtype=F32) / half)
    ang = jnp.arange(s, dtype=F32)[:, None] * inv[None, :]
    cos = jnp.cos(ang)
    sin = jnp.sin(ang)
    ones = jnp.ones((s, QK_NOPE), F32)
    tail = HEAD_PAD - QK_NOPE - QK_ROPE
    cos_t = jnp.concatenate([ones, cos, cos, jnp.ones((s, tail), F32)], axis=1)
    sin_t = jnp.concatenate([0 * ones, -sin, sin, jnp.zeros((s, tail), F32)], axis=1)
    return cos_t, sin_t


def _swap_halves(w):
    half = QK_ROPE // 2
    return jnp.concatenate([w[..., half:], w[..., :half]], axis=-1)


def _prep_mixer(w_in, w_uq, w_ukv, w_out):
    d = w_in.shape[0]
    q_rank = w_uq.shape[0]
    kv_rank = w_ukv.shape[0]
    f_w = FNET_GROUPS * FNET_CH
    tail = HEAD_PAD - QK_NOPE - QK_ROPE
    o = q_rank + kv_rank
    w_kr = w_in[:, o:o + QK_ROPE]
    pad_rope = lambda w: jnp.concatenate(
        [jnp.zeros(w.shape[:-1] + (QK_NOPE,), F32), w, jnp.zeros(w.shape[:-1] + (tail,), F32)], axis=-1)
    win = jnp.concatenate([w_in[:, :o], w_in[:, o + QK_ROPE:], pad_rope(w_kr), pad_rope(_swap_halves(w_kr))],
                          axis=1).astype(BF16)
    uq = w_uq.reshape(q_rank, MLA_HEADS, QK_NOPE + QK_ROPE)
    zt = jnp.zeros((q_rank, MLA_HEADS, tail), F32)
    wuq = jnp.concatenate([uq, zt], axis=-1).reshape(q_rank, -1).astype(BF16)
    wuqr = jnp.concatenate([jnp.zeros((q_rank, MLA_HEADS, QK_NOPE), F32), _swap_halves(uq[..., QK_NOPE:]), zt],
                           axis=-1).reshape(q_rank, -1).astype(BF16)
    ukv = w_ukv.reshape(kv_rank, MLA_HEADS, QK_NOPE + V_HEAD)
    wuk = jnp.concatenate([ukv[..., :QK_NOPE], jnp.zeros((kv_rank, MLA_HEADS, HEAD_PAD - QK_NOPE), F32)],
                          axis=-1).reshape(kv_rank, -1).astype(BF16)
    wuvt = jnp.concatenate([ukv[..., QK_NOPE:], jnp.zeros((kv_rank, MLA_HEADS, VT_ROWS - V_HEAD), F32)],
                           axis=-1).reshape(kv_rank, -1).T.astype(BF16)
    ones = (jnp.arange(MLA_HEADS * VT_ROWS) % VT_ROWS == V_HEAD).astype(F32).reshape(-1, 1)
    av = MLA_HEADS * V_HEAD
    return dict(win=win, wuq=wuq, wuqr=wuqr, wuk=wuk, wuvt=wuvt, ones=ones,
                wa=w_out[:av].astype(BF16), wf=w_out[av:].astype(BF16),
                q_rank=q_rank, kv_rank=kv_rank, f_w=f_w)


def _row(v):
    return v.reshape(1, -1).astype(F32)


def _trunk(x, mem, p):
    depth = p['g_mix'].shape[0]
    s = x.shape[1]
    for i in range(depth):
        j = i // 2
        if i % 2 == 0:
            mp = p['mixer'][j]
            cos_t, sin_t = _rope_tables(s)
            q_scale = (QK_NOPE + QK_ROPE) ** -0.5 * math.log2(math.e)
            q, k, vt, f = _mixer_in(x, _row(p['g_mix'][i]), mp['win'], _row(p['mla_g_q'][j]), mp['wuq'], mp['wuqr'],
                                    _row(p['mla_g_kv'][j]), mp['wuk'], mp['wuvt'], mp['ones'], cos_t, sin_t,
                                    q_rank=mp['q_rank'], kv_rank=mp['kv_rank'], f_w=mp['f_w'], q_scale=q_scale)
            attn_t = _flash(q, k, vt)
            fo = _fnet(f)
            x = _mixer_out(x, attn_t, fo, mp['wa'], mp['wf'])
        else:
            u = _conv_in(x, _row(p['g_mix'][i]), p['conv_w_pw1'][j], _row(p['conv_b_pw1'][j]))
            x = _conv_out(x, u, p['conv_w_dw'][j], _row(p['conv_b_dw'][j]), _row(p['conv_g_ln'][j]),
                          _row(p['conv_b_ln'][j]), p['conv_w_pw2'][j], _row(p['conv_b_pw2'][j]))
        kv = _mem_kv(mem, _row(p['g_xkv'][i]), p['xa_wkv'][i])
        x = _xattn(x, _row(p['g_xq'][i]), p['xa_wq'][i], kv, p['xa_wo'][i])
        x = _ffn(x, _row(p['g_ffn'][i]), p['ffn_w1'][i], p['ffn_w3'][i], p['ffn_w2'][i],
                 _row(p['g_final']), final_norm=(i == depth - 1))
    return x


def kernel(x_prompt, x_sample, mem_prompt, mem_sample, g_mix, g_xq, g_xkv, xa_wq, xa_wk, xa_wv, xa_wo, g_ffn, ffn_w1, ffn_w3, ffn_w2, ab_w_in, mla_g_q, mla_w_uq, mla_g_kv, mla_w_ukv, ab_w_out, conv_w_pw1, conv_b_pw1, conv_w_dw, conv_b_dw, conv_g_ln, conv_b_ln, conv_w_pw2, conv_b_pw2, g_final):
    p = {
        'g_mix': g_mix, 'g_xq': g_xq, 'g_xkv': g_xkv, 'g_ffn': g_ffn, 'g_final': g_final,
        'xa_wq': xa_wq.astype(BF16), 'xa_wo': xa_wo.astype(BF16),
        'xa_wkv': jnp.concatenate([xa_wk, xa_wv], axis=-1).astype(BF16),
        'ffn_w1': ffn_w1.astype(BF16), 'ffn_w3': ffn_w3.astype(BF16), 'ffn_w2': ffn_w2.astype(BF16),
        'mla_g_q': mla_g_q, 'mla_g_kv': mla_g_kv,
        'mixer': [_prep_mixer(ab_w_in[j], mla_w_uq[j], mla_w_ukv[j], ab_w_out[j])
                  for j in range(ab_w_in.shape[0])],
        'conv_w_pw1': conv_w_pw1.astype(BF16), 'conv_b_pw1': conv_b_pw1,
        'conv_w_dw': conv_w_dw, 'conv_b_dw': conv_b_dw, 'conv_g_ln': conv_g_ln, 'conv_b_ln': conv_b_ln,
        'conv_w_pw2': conv_w_pw2.astype(BF16), 'conv_b_pw2': conv_b_pw2,
    }
    return (_trunk(x_prompt, mem_prompt, p), _trunk(x_sample, mem_sample, p))
```

```python
import functools
import math

import jax
import jax.numpy as jnp
from jax import lax
from jax.experimental import pallas as pl
from jax.experimental.pallas import tpu as pltpu

MLA_HEADS = 8
QK_NOPE = 64
QK_ROPE = 32
V_HEAD = 64
ROPE_THETA = 10000.0
FNET_GROUPS = 4
FNET_CH = 128
XA_HEADS = 4
EPS = 1e-6

LANES = 128
HEAD_PAD = 128
VT_ROWS = 80
KV_CHUNK = 512
SCORE_SLOTS = 4
FF_CHUNK = 256
VMEM_LIMIT = 52 * 1024 * 1024

F32 = jnp.float32
BF16 = jnp.bfloat16


def _params(*sem):
    return pltpu.CompilerParams(dimension_semantics=sem, vmem_limit_bytes=VMEM_LIMIT)


def _rms(x, g):
    ms = jnp.mean(x * x, axis=-1, keepdims=True)
    return x * lax.rsqrt(ms + EPS) * g


def _dot(a, b):
    return jnp.dot(a, b, preferred_element_type=F32)


def _const_spec(shape):
    nd = len(shape)
    return pl.BlockSpec(shape, lambda *_: (0,) * nd)


def _tile(n, want):
    t = min(n, want)
    assert n % t == 0, (n, t)
    return t


def _mixer_in_kernel(x_ref, g_ref, win_ref, gq_ref, wuq_ref, wuqr_ref, gkv_ref, wuk_ref, wuvt_ref, ones_ref,
                     cos_ref, sin_ref, q_ref, k_ref, vt_ref, f_ref, *, q_rank, kv_rank, f_w, q_scale):
    x = x_ref[0]
    h = _rms(x, g_ref[...]).astype(BF16)
    z = _dot(h, win_ref[...])
    o = 0
    cq = z[:, o:o + q_rank]; o += q_rank
    ckv = z[:, o:o + kv_rank]; o += kv_rank
    f_ref[0] = z[:, o:o + f_w]; o += f_w
    kr = z[:, o:o + HEAD_PAD]; o += HEAD_PAD
    krr = z[:, o:o + HEAD_PAD]
    cos = cos_ref[...]
    sin = sin_ref[...]
    cqn = _rms(cq, gq_ref[...]).astype(BF16)
    q = _dot(cqn, wuq_ref[...])
    qr = _dot(cqn, wuqr_ref[...])
    ckvn = _rms(ckv, gkv_ref[...]).astype(BF16)
    k = _dot(ckvn, wuk_ref[...])
    vt = lax.dot_general(wuvt_ref[...], ckvn, (((1,), (1,)), ((), ())), preferred_element_type=F32)
    vt = vt + ones_ref[...]
    tk = vt_ref.shape[-1]
    for hd in range(MLA_HEADS):
        for c in range(vt_ref.shape[2]):
            vt_ref[0, hd, c] = vt[hd * VT_ROWS:(hd + 1) * VT_ROWS, c * tk:(c + 1) * tk].astype(BF16)
    k_rope = kr * cos + krr * sin
    for hd in range(MLA_HEADS):
        sl = slice(hd * HEAD_PAD, (hd + 1) * HEAD_PAD)
        q_ref[0, :, sl] = ((q[:, sl] * cos + qr[:, sl] * sin) * q_scale).astype(BF16)
        k_ref[0, :, sl] = (k[:, sl] + k_rope).astype(BF16)


def _mixer_in(x, g, win, gq, wuq, wuqr, gkv, wuk, wuvt, ones, cos, sin, *, q_rank, kv_rank, f_w, q_scale):
    b, s, d = x.shape
    tm = _tile(s, 512)
    tk = _tile(tm, KV_CHUNK)
    cpt = tm // tk
    hw = MLA_HEADS * HEAD_PAD
    tok = lambda w: pl.BlockSpec((1, tm, w), lambda bi, i: (bi, i, 0))
    rope = pl.BlockSpec((tm, HEAD_PAD), lambda bi, i: (i, 0))
    kern = functools.partial(_mixer_in_kernel, q_rank=q_rank, kv_rank=kv_rank, f_w=f_w, q_scale=q_scale)
    return pl.pallas_call(
        kern,
        grid=(b, s // tm),
        in_specs=[tok(d), _const_spec(g.shape), _const_spec(win.shape), _const_spec(gq.shape),
                  _const_spec(wuq.shape), _const_spec(wuqr.shape), _const_spec(gkv.shape),
                  _const_spec(wuk.shape), _const_spec(wuvt.shape), _const_spec(ones.shape), rope, rope],
        out_specs=[tok(hw), tok(hw),
                   pl.BlockSpec((1, MLA_HEADS, cpt, VT_ROWS, tk), lambda bi, i: (bi, 0, i, 0, 0)),
                   tok(f_w)],
        out_shape=[jax.ShapeDtypeStruct((b, s, hw), BF16), jax.ShapeDtypeStruct((b, s, hw), BF16),
                   jax.ShapeDtypeStruct((b, MLA_HEADS, s // tk, VT_ROWS, tk), BF16),
                   jax.ShapeDtypeStruct((b, s, f_w), F32)],
        compiler_params=_params("parallel", "parallel"),
        name="mixer_in",
    )(x, g, win, gq, wuq, wuqr, gkv, wuk, wuvt, ones, cos, sin)


def _q_transposed(q_ref, t, tq):
    return q_ref[0, t * tq:(t + 1) * tq, :].astype(F32).T.astype(BF16)


def _scores(k_ref, s_ref, qt, row0, slot):
    tk = s_ref.shape[1]
    s = _dot(k_ref[0, pl.ds(row0, tk), :], qt)
    s_ref[slot] = s
    return jnp.max(s, axis=0, keepdims=True)


def _softmax_update(vt_ref, s_ref, j, slot, m, acc, cmax):
    m_new = jnp.maximum(m, cmax)
    alpha = jnp.exp2(m - m_new)
    p = jnp.exp2(s_ref[slot] - m_new).astype(BF16)
    return m_new, alpha * acc + _dot(vt_ref[0, 0, j], p)


def _attn_init(tq):
    return jnp.full((1, tq), -jnp.inf, F32), jnp.zeros((VT_ROWS, tq), F32)


def _attn_out(acc):
    return (acc[:V_HEAD] / acc[V_HEAD:V_HEAD + 1]).astype(BF16)


def _flash_kernel(q_ref, k_ref, vt_ref, o_ref, s_ref, *, tq):
    nk, _, tk = vt_ref.shape[2:]
    nt = q_ref.shape[1] // tq
    nslots = s_ref.shape[0]
    assert nslots >= 3 and nk % nslots == 0, (nk, nslots)

    def tile(t, qt, qt_next, ca, cb):
        col0 = t * tq if isinstance(t, int) else pl.multiple_of(t * tq, tq)
        m, acc = _attn_init(tq)
        for u in range(nk):
            if u + 2 < nk:
                cn = _scores(k_ref, s_ref, qt, (u + 2) * tk, (u + 2) % nslots)
            elif qt_next is not None:
                cn = _scores(k_ref, s_ref, qt_next, (u + 2 - nk) * tk, (u + 2 - nk) % nslots)
            else:
                cn = None
            m, acc = _softmax_update(vt_ref, s_ref, u, u % nslots, m, acc, ca)
            ca, cb = cb, cn
        o_ref[0, :, pl.ds(col0, tq)] = _attn_out(acc)
        return ca, cb

    qt = _q_transposed(q_ref, 0, tq)
    c0 = _scores(k_ref, s_ref, qt, 0, 0)
    c1 = _scores(k_ref, s_ref, qt, tk, 1)

    def body(t, carry):
        qt, ca, cb = carry
        qt_next = q_ref[0, pl.ds(pl.multiple_of((t + 1) * tq, tq), tq), :].astype(F32).T.astype(BF16)
        ca, cb = tile(t, qt, qt_next, ca, cb)
        return qt_next, ca, cb

    qt, ca, cb = lax.fori_loop(0, nt - 1, body, (qt, c0, c1))
    tile(nt - 1, qt, None, ca, cb)


def _flash(q, k, vt):
    b, s, _ = q.shape
    tq = _tile(s, 1024)
    _, _, nk, _, tk = vt.shape
    return pl.pallas_call(
        functools.partial(_flash_kernel, tq=tq),
        grid=(b, MLA_HEADS),
        in_specs=[pl.BlockSpec((1, s, HEAD_PAD), lambda bi, h: (bi, 0, h)),
                  pl.BlockSpec((1, s, HEAD_PAD), lambda bi, h: (bi, 0, h)),
                  pl.BlockSpec((1, 1, nk, VT_ROWS, tk), lambda bi, h: (bi, h, 0, 0, 0))],
        out_specs=pl.BlockSpec((1, V_HEAD, s), lambda bi, h: (bi, h, 0)),
        out_shape=jax.ShapeDtypeStruct((b, MLA_HEADS * V_HEAD, s), BF16),
        scratch_shapes=[pltpu.VMEM((SCORE_SLOTS, tk, tq), F32)],
        compiler_params=_params("parallel", "parallel"),
        name="mla_flash",
    )(q, k, vt)


def _split(x):
    hi = lax.bitcast_convert_type(lax.bitcast_convert_type(x, jnp.uint32) & jnp.uint32(0xFFFF0000), F32)
    return hi.astype(BF16), (x - hi).astype(BF16)


def _split_stack(w, axis):
    hi, lo = _split(w)
    return jnp.concatenate([hi, lo], axis=axis)


def _const_times_data(wcat, x):
    m = wcat.shape[0] // 2
    x_hi, x_lo = _split(x)
    r = _dot(wcat, x_hi)
    return r[:m] + r[m:] + _dot(wcat[:m], x_lo)


def _dft_a_kernel(x_ref, w_ref, ar_ref, ai_ref):
    n1 = x_ref.shape[1]
    r = _const_times_data(w_ref[...], x_ref[0])
    ar_ref[0] = r[:n1]
    ai_ref[0] = r[n1:]


def _dft_a(x2, wcat):
    b, n1, w = x2.shape
    tc = _tile(w, 4096)
    blk = pl.BlockSpec((1, n1, tc), lambda bi, j: (bi, 0, j))
    sds = jax.ShapeDtypeStruct((b, n1, w), F32)
    return pl.pallas_call(
        _dft_a_kernel,
        grid=(b, w // tc),
        in_specs=[blk, _const_spec(wcat.shape)],
        out_specs=[blk, blk],
        out_shape=[sds, sds],
        compiler_params=_params("parallel", "parallel"),
        name="fnet_dft_a",
    )(x2, wcat)


def _dft_b_kernel(ar_ref, ai_ref, g_ref, wc_ref, o_ref, *, kb, scale):
    n2, w = ar_ref.shape[2:]
    wc = wc_ref[...]
    for j in range(kb):
        a = jnp.concatenate([ar_ref[0, j], ai_ref[0, j]], axis=1)
        p = _const_times_data(g_ref[j], a)
        yr = p[:n2, :w] - p[n2:, w:]
        yi = p[:n2, w:] + p[n2:, :w]
        lhs = jnp.concatenate(
            [jnp.concatenate([yr[:, g * FNET_CH:(g + 1) * FNET_CH], yi[:, g * FNET_CH:(g + 1) * FNET_CH]], axis=1)
             for g in range(FNET_GROUPS)], axis=0)
        l_hi, l_lo = _split(lhs)
        r = _dot(l_hi, wc)
        res = (r[:, :FNET_CH] + r[:, FNET_CH:] + _dot(l_lo, wc[:, :FNET_CH])) * scale
        for g in range(FNET_GROUPS):
            o_ref[0, :, j, g * FNET_CH:(g + 1) * FNET_CH] = res[g * n2:(g + 1) * n2]


def _dft_b(ar, ai, gcat, wc, scale):
    b, n1, n2, w = ar.shape
    kb = 8
    assert n1 % kb == 0
    a_blk = pl.BlockSpec((1, kb, n2, w), lambda bi, i: (bi, i, 0, 0))
    g_blk = pl.BlockSpec((kb,) + gcat.shape[1:], lambda bi, i: (i, 0, 0))
    return pl.pallas_call(
        functools.partial(_dft_b_kernel, kb=kb, scale=scale),
        grid=(b, n1 // kb),
        in_specs=[a_blk, a_blk, g_blk, _const_spec(wc.shape)],
        out_specs=pl.BlockSpec((1, n2, kb, w), lambda bi, i: (bi, 0, i, 0)),
        out_shape=jax.ShapeDtypeStruct((b, n2, n1, w), F32),
        compiler_params=_params("parallel", "parallel"),
        name="fnet_dft_b",
    )(ar, ai, gcat, wc)


def _dft_tables(s):
    n1 = 1 << ((s.bit_length() - 1 + 1) // 2)
    n2 = s // n1
    assert n1 * n2 == s and n1 % 8 == 0, (s, n1, n2)

    def cs(idx, period):
        ang = (2.0 * math.pi / period) * (idx % period).astype(F32)
        return jnp.cos(ang), jnp.sin(ang)

    i1 = jnp.arange(n1, dtype=jnp.int32)
    i2 = jnp.arange(n2, dtype=jnp.int32)
    c1, s1 = cs(i1[:, None] * i1[None, :], n1)
    wa = _split_stack(jnp.concatenate([c1, -s1], axis=0), 0)
    c2, s2 = cs(i2[:, None] * i2[None, :], n2)
    ct, st = cs(i1[:, None] * i2[None, :], s)
    c2, s2, ct, st = lax.optimization_barrier((c2, s2, ct, st))
    gr =c2[None] * ct[:, None, :] - s2[None] * st[:, None, :]
    gi = -(s2[None] * ct[:, None, :] + c2[None] * st[:, None, :])
    gcat = _split_stack(jnp.concatenate([gr, gi], axis=1), 1)
    ic = jnp.arange(FNET_CH, dtype=jnp.int32)
    cc, sc = cs(ic[:, None] * ic[None, :], FNET_CH)
    wc = _split_stack(jnp.concatenate([cc, sc], axis=0), 1)
    return n1, n2, wa, gcat, wc


def _fnet(f):
    b, s, w = f.shape
    n1, n2, wa, gcat, wc = _dft_tables(s)
    ar, ai = _dft_a(f.reshape(b, n1, n2 * w), wa)
    out = _dft_b(ar.reshape(b, n1, n2, w), ai.reshape(b, n1, n2, w), gcat, wc,
                 1.0 / math.sqrt(s * FNET_CH))
    return out.reshape(b, s, w)


def _mixer_out_kernel(x_ref, at_ref, f_ref, wa_ref, wf_ref, o_ref):
    attn = lax.dot_general(at_ref[0], wa_ref[...], (((0,), (0,)), ((), ())), preferred_element_type=F32)
    o_ref[0] = x_ref[0] + attn + _dot(f_ref[0].astype(BF16), wf_ref[...])


def _mixer_out(x, attn_t, fo, wa, wf):
    b, s, d = x.shape
    tm = _tile(s, 512)
    tok = lambda w: pl.BlockSpec((1, tm, w), lambda bi, i: (bi, i, 0))
    return pl.pallas_call(
        _mixer_out_kernel,
        grid=(b, s // tm),
        in_specs=[tok(d), pl.BlockSpec((1, attn_t.shape[1], tm), lambda bi, i: (bi, 0, i)),
                  tok(fo.shape[-1]), _const_spec(wa.shape), _const_spec(wf.shape)],
        out_specs=tok(d),
        out_shape=jax.ShapeDtypeStruct(x.shape, F32),
        compiler_params=_params("parallel", "parallel"),
        name="mixer_out",
    )(x, attn_t, fo, wa, wf)


def _conv_in_kernel(x_ref, g_ref, w_ref, b_ref, u_ref):
    d = x_ref.shape[-1]
    h = _rms(x_ref[0], g_ref[...]).astype(BF16)
    z = _dot(h, w_ref[...]) + b_ref[...]
    u_ref[0] = z[:, :d] * jax.nn.sigmoid(z[:, d:])


def _conv_in(x, g, w, bias):
    b, s, d = x.shape
    tm = _tile(s, 512)
    tok = pl.BlockSpec((1, tm, d), lambda bi, i: (bi, i, 0))
    return pl.pallas_call(
        _conv_in_kernel,
        grid=(b, s // tm),
        in_specs=[tok, _const_spec(g.shape), _const_spec(w.shape), _const_spec(bias.shape)],
        out_specs=tok,
        out_shape=jax.ShapeDtypeStruct(x.shape, F32),
        compiler_params=_params("parallel", "parallel"),
        name="conv_in",
    )(x, g, w, bias)


HALO = 16


def _conv_out_kernel(x_ref, u_ref, ul_ref, ur_ref, wdw_ref, bdw_ref, gln_ref, bln_ref, w2_ref, b2_ref,
                     o_ref, buf_ref, c_ref, *, taps, rows):
    i = pl.program_id(1)
    n = pl.num_programs(1)
    tm = u_ref.shape[1]
    d = u_ref.shape[2]
    half = taps // 2
    nlb = d // LANES
    for cb in range(nlb):
        ls = slice(cb * LANES, (cb + 1) * LANES)
        buf_ref[cb, 0:HALO] = jnp.where(i > 0, ul_ref[0, :, ls], 0.0)
        buf_ref[cb, HALO:HALO + tm] = u_ref[0, :, ls]
        buf_ref[cb, HALO + tm:HALO + tm + HALO] = jnp.where(i < n - 1, ur_ref[0, :, ls], 0.0)

    def lane_block(cb, _):
        for r0 in range(0, tm, rows):
            acc = jnp.zeros((rows, LANES), F32)
            for t in range(taps):
                lo = r0 + HALO - half + t
                acc = acc + buf_ref[cb, lo:lo + rows] * wdw_ref[cb, t:t + 1]
            c_ref[cb, r0:r0 + rows] = acc
        return 0

    lax.fori_loop(0, nlb, lane_block, 0)
    c = jnp.concatenate([c_ref[cb] for cb in range(nlb)], axis=1) + bdw_ref[...]
    mu = jnp.mean(c, axis=-1, keepdims=True)
    cc = c - mu
    y = cc * lax.rsqrt(jnp.mean(cc * cc, axis=-1, keepdims=True) + EPS)
    y = y * gln_ref[...] + bln_ref[...]
    y = y * jax.nn.sigmoid(y)
    o_ref[0] = x_ref[0] + _dot(y.astype(BF16), w2_ref[...]) + b2_ref[...]


def _conv_out(x, u, wdw, bdw, gln, bln, w2, b2):
    b, s, d = x.shape
    taps = wdw.shape[0]
    assert taps // 2 <= HALO
    tm = _tile(s, 256)
    hb = tm // HALO
    nh = s // HALO
    tok = pl.BlockSpec((1, tm, d), lambda bi, i: (bi, i, 0))
    left = pl.BlockSpec((1, HALO, d), lambda bi, i: (bi, jnp.maximum(i * hb - 1, 0), 0))
    right = pl.BlockSpec((1, HALO, d), lambda bi, i: (bi, jnp.minimum((i + 1) * hb, nh - 1), 0))
    kern = functools.partial(_conv_out_kernel, taps=taps, rows=32)
    nlb = d // LANES
    wdw = wdw.reshape(taps, nlb, LANES).transpose(1, 0, 2)
    return pl.pallas_call(
        kern,
        grid=(b, s // tm),
        in_specs=[tok, tok, left, right, _const_spec(wdw.shape), _const_spec(bdw.shape),
                  _const_spec(gln.shape), _const_spec(bln.shape), _const_spec(w2.shape), _const_spec(b2.shape)],
        out_specs=tok,
        out_shape=jax.ShapeDtypeStruct(x.shape, F32),
        scratch_shapes=[pltpu.VMEM((nlb, tm + 2 * HALO, LANES), F32), pltpu.VMEM((nlb, tm, LANES), F32)],
        compiler_params=_params("parallel", "parallel"),
        name="conv_out",
    )(x, u, u, u, wdw, bdw, gln, bln, w2, b2)


def _mem_kv_kernel(m_ref, g_ref, w_ref, o_ref):
    h = _rms(m_ref[0], g_ref[...]).astype(BF16)
    o_ref[0] = _dot(h, w_ref[...]).astype(BF16)


def _mem_kv(mem, g, wkv):
    b, nm, d = mem.shape
    n = wkv.shape[1]
    return pl.pallas_call(
        _mem_kv_kernel,
        grid=(b,),
        in_specs=[pl.BlockSpec((1, nm, d), lambda bi: (bi, 0, 0)), _const_spec(g.shape), _const_spec(wkv.shape)],
        out_specs=pl.BlockSpec((1, nm, n), lambda bi: (bi, 0, 0)),
        out_shape=jax.ShapeDtypeStruct((b, nm, n), BF16),
        compiler_params=_params("parallel"),
        name="mem_kv",
    )(mem, g, wkv)


def _xattn_kernel(x_ref, g_ref, wq_ref, kv_ref, wo_ref, o_ref, *, scale):
    x = x_ref[0]
    d = x.shape[-1]
    hd = d // XA_HEADS
    h = _rms(x, g_ref[...]).astype(BF16)
    q = (_dot(h, wq_ref[...]) * scale).astype(BF16)
    outs = []
    for a in range(XA_HEADS):
        k = kv_ref[0, :, a * hd:(a + 1) * hd]
        v = kv_ref[0, :, d + a * hd:d + (a + 1) * hd]
        s = lax.dot_general(q[:, a * hd:(a + 1) * hd], k, (((1,), (1,)), ((), ())),
                            preferred_element_type=F32)
        p = jnp.exp(s - jnp.max(s, axis=1, keepdims=True))
        l = jnp.sum(p, axis=1, keepdims=True)
        outs.append((_dot(p.astype(BF16), v) / l).astype(BF16))
    o = jnp.concatenate(outs, axis=1)
    o_ref[0] = x + _dot(o, wo_ref[...])


def _xattn(x, g, wq, kv, wo):
    b, s, d = x.shape
    tm = _tile(s, 512)
    tok = pl.BlockSpec((1, tm, d), lambda bi, i: (bi, i, 0))
    return pl.pallas_call(
        functools.partial(_xattn_kernel, scale=(d // XA_HEADS) ** -0.5),
        grid=(b, s // tm),
        in_specs=[tok, _const_spec(g.shape), _const_spec(wq.shape),
                  pl.BlockSpec((1,) + kv.shape[1:], lambda bi, i: (bi, 0, 0)), _const_spec(wo.shape)],
        out_specs=tok,
        out_shape=jax.ShapeDtypeStruct(x.shape, F32),
        compiler_params=_params("parallel", "parallel"),
        name="mem_xattn",
    )(x, g, wq, kv, wo)


def _ffn_kernel(x_ref, g_ref, w1_ref, w3_ref, w2_ref, gf_ref, o_ref, *, final_norm):
    x = x_ref[0]
    h = _rms(x, g_ref[...]).astype(BF16)
    acc = x
    for c0 in range(0, w1_ref.shape[1], FF_CHUNK):
        a = _dot(h, w1_ref[:, c0:c0 + FF_CHUNK])
        bb = _dot(h, w3_ref[:, c0:c0 + FF_CHUNK])
        acc = acc + _dot((a * jax.nn.sigmoid(a) * bb).astype(BF16), w2_ref[c0:c0 + FF_CHUNK, :])
    if final_norm:
        acc = _rms(acc, gf_ref[...])
    o_ref[0] = acc


def _ffn(x, g, w1, w3, w2, gf, final_norm):
    b, s, d = x.shape
    tm = _tile(s, 512)
    tok = pl.BlockSpec((1, tm, d), lambda bi, i: (bi, i, 0))
    assert w1.shape[1] % FF_CHUNK == 0
    wspec = lambda w: pl.BlockSpec(w.shape, lambda bi, i: (0, 0), pipeline_mode=pl.Buffered(1))
    return pl.pallas_call(
        functools.partial(_ffn_kernel, final_norm=final_norm),
        grid=(b, s // tm),
        in_specs=[tok, _const_spec(g.shape), wspec(w1), wspec(w3), wspec(w2), _const_spec(gf.shape)],
        out_specs=tok,
        out_shape=jax.ShapeDtypeStruct(x.shape, F32),
        compiler_params=_params("parallel", "parallel"),
        name="swiglu_ffn",
    )(x, g, w1, w3, w2, gf)


def _rope_tables(s):
    half = QK_ROPE // 2
    inv = ROPE_THETA ** (-jnp.arange(half, dtype=F32) / half)
    ang = jnp.arange(s, dtype=F32)[:, None] * inv[None, :]
    cos = jnp.cos(ang)
    sin = jnp.sin(ang)
    ones = jnp.ones((s, QK_NOPE), F32)
    tail = HEAD_PAD - QK_NOPE - QK_ROPE
    cos_t = jnp.concatenate([ones, cos, cos, jnp.ones((s, tail), F32)], axis=1)
    sin_t = jnp.concatenate([0 * ones, -sin, sin, jnp.zeros((s, tail), F32)], axis=1)
    return cos_t, sin_t


def _swap_halves(w):
    half = QK_ROPE // 2
    return jnp.concatenate([w[..., half:], w[..., :half]], axis=-1)


def _prep_mixer(w_in, w_uq, w_ukv, w_out):
    d = w_in.shape[0]
    q_rank = w_uq.shape[0]
    kv_rank = w_ukv.shape[0]
    f_w = FNET_GROUPS * FNET_CH
    tail = HEAD_PAD - QK_NOPE - QK_ROPE
    o = q_rank + kv_rank
    w_kr = w_in[:, o:o + QK_ROPE]
    pad_rope = lambda w: jnp.concatenate(
        [jnp.zeros(w.shape[:-1] + (QK_NOPE,), F32), w, jnp.zeros(w.shape[:-1] + (tail,), F32)], axis=-1)
    win = jnp.concatenate([w_in[:, :o], w_in[:, o + QK_ROPE:], pad_rope(w_kr), pad_rope(_swap_halves(w_kr))],
                          axis=1).astype(BF16)
    uq = w_uq.reshape(q_rank, MLA_HEADS, QK_NOPE + QK_ROPE)
    zt = jnp.zeros((q_rank, MLA_HEADS, tail), F32)
    wuq = jnp.concatenate([uq, zt], axis=-1).reshape(q_rank, -1).astype(BF16)
    wuqr = jnp.concatenate([jnp.zeros((q_rank, MLA_HEADS, QK_NOPE), F32), _swap_halves(uq[..., QK_NOPE:]), zt],
                           axis=-1).reshape(q_rank, -1).astype(BF16)
    ukv = w_ukv.reshape(kv_rank, MLA_HEADS, QK_NOPE + V_HEAD)
    wuk = jnp.concatenate([ukv[..., :QK_NOPE], jnp.zeros((kv_rank, MLA_HEADS, HEAD_PAD - QK_NOPE), F32)],
                          axis=-1).reshape(kv_rank, -1).astype(BF16)
    wuvt = jnp.concatenate([ukv[..., QK_NOPE:], jnp.zeros((kv_rank, MLA_HEADS, VT_ROWS - V_HEAD), F32)],
                           axis=-1).reshape(kv_rank, -1).T.astype(BF16)
    ones = (jnp.arange(MLA_HEADS * VT_ROWS) % VT_ROWS == V_HEAD).astype(F32).reshape(-1, 1)
    av = MLA_HEADS * V_HEAD
    return dict(win=win, wuq=wuq, wuqr=wuqr, wuk=wuk, wuvt=wuvt, ones=ones,
                wa=w_out[:av].astype(BF16), wf=w_out[av:].astype(BF16),
                q_rank=q_rank, kv_rank=kv_rank, f_w=f_w)


def _row(v):
    return v.reshape(1, -1).astype(F32)


def _trunk(x, mem, p):
    depth = p['g_mix'].shape[0]
    s = x.shape[1]
    for i in range(depth):
        j = i // 2
        if i % 2 == 0:
            mp = p['mixer'][j]
            cos_t, sin_t = _rope_tables(s)
            q_scale = (QK_NOPE + QK_ROPE) ** -0.5 * math.log2(math.e)
            q, k, vt, f = _mixer_in(x, _row(p['g_mix'][i]), mp['win'], _row(p['mla_g_q'][j]), mp['wuq'], mp['wuqr'],
                                    _row(p['mla_g_kv'][j]), mp['wuk'], mp['wuvt'], mp['ones'], cos_t, sin_t,
                                    q_rank=mp['q_rank'], kv_rank=mp['kv_rank'], f_w=mp['f_w'], q_scale=q_scale)
            attn_t = _flash(q, k, vt)
            fo = _fnet(f)
            x = _mixer_out(x, attn_t, fo, mp['wa'], mp['wf'])
        else:
            u = _conv_in(x, _row(p['g_mix'][i]), p['conv_w_pw1'][j], _row(p['conv_b_pw1'][j]))
            x = _conv_out(x, u, p['conv_w_dw'][j], _row(p['conv_b_dw'][j]), _row(p['conv_g_ln'][j]),
                          _row(p['conv_b_ln'][j]), p['conv_w_pw2'][j], _row(p['conv_b_pw2'][j]))
        kv = _mem_kv(mem, _row(p['g_xkv'][i]), p['xa_wkv'][i])
        x = _xattn(x, _row(p['g_xq'][i]), p['xa_wq'][i], kv, p['xa_wo'][i])
        x = _ffn(x, _row(p['g_ffn'][i]), p['ffn_w1'][i], p['ffn_w3'][i], p['ffn_w2'][i],
                 _row(p['g_final']), final_norm=(i == depth - 1))
    return x


def kernel(x_prompt, x_sample, mem_prompt, mem_sample, g_mix, g_xq, g_xkv, xa_wq, xa_wk, xa_wv, xa_wo, g_ffn, ffn_w1, ffn_w3, ffn_w2, ab_w_in, mla_g_q, mla_w_uq, mla_g_kv, mla_w_ukv, ab_w_out, conv_w_pw1, conv_b_pw1, conv_w_dw, conv_b_dw, conv_g_ln, conv_b_ln, conv_w_pw2, conv_b_pw2, g_final):
    p = {
        'g_mix': g_mix, 'g_xq': g_xq, 'g_xkv': g_xkv, 'g_ffn': g_ffn, 'g_final': g_final,
        'xa_wq': xa_wq.astype(BF16), 'xa_wo': xa_wo.astype(BF16),
        'xa_wkv': jnp.concatenate([xa_wk, xa_wv], axis=-1).astype(BF16),
        'ffn_w1': ffn_w1.astype(BF16), 'ffn_w3': ffn_w3.astype(BF16), 'ffn_w2': ffn_w2.astype(BF16),
        'mla_g_q': mla_g_q, 'mla_g_kv': mla_g_kv,
        'mixer': [_prep_mixer(ab_w_in[j], mla_w_uq[j], mla_w_ukv[j], ab_w_out[j])
                  for j in range(ab_w_in.shape[0])],
        'conv_w_pw1': conv_w_pw1.astype(BF16), 'conv_b_pw1': conv_b_pw1,
        'conv_w_dw': conv_w_dw, 'conv_b_dw': conv_b_dw, 'conv_g_ln': conv_g_ln, 'conv_b_ln': conv_b_ln,
        'conv_w_pw2': conv_w_pw2.astype(BF16), 'conv_b_pw2': conv_b_pw2,
    }
    return (_trunk(x_prompt, mem_prompt, p), _trunk(x_sample, mem_sample, p))
```

```python
import functools
import math

import jax
import jax.numpy as jnp
from jax import lax
from jax.experimental import pallas as pl
from jax.experimental.pallas import tpu as pltpu

MLA_HEADS = 8
QK_NOPE = 64
QK_ROPE = 32
V_HEAD = 64
ROPE_THETA = 10000.0
FNET_GROUPS = 4
FNET_CH = 128
XA_HEADS = 4
EPS = 1e-6

LANES = 128
HEAD_PAD = 128
VT_ROWS = 80
KV_CHUNK = 512
SCORE_SLOTS = 4
FF_CHUNK = 256
VMEM_LIMIT = 52 * 1024 * 1024

F32 = jnp.float32
BF16 = jnp.bfloat16


def _params(*sem):
    return pltpu.CompilerParams(dimension_semantics=sem, vmem_limit_bytes=VMEM_LIMIT)


def _rms(x, g):
    ms = jnp.mean(x * x, axis=-1, keepdims=True)
    return x * lax.rsqrt(ms + EPS) * g


def _dot(a, b):
    return jnp.dot(a, b, preferred_element_type=F32)


def _const_spec(shape):
    nd = len(shape)
    return pl.BlockSpec(shape, lambda *_: (0,) * nd)


def _tile(n, want):
    t = min(n, want)
    assert n % t == 0, (n, t)
    return t


def _mixer_in_kernel(x_ref, g_ref, win_ref, gq_ref, wuq_ref, wuqr_ref, gkv_ref, wuk_ref, wuvt_ref, ones_ref,
                     cos_ref, sin_ref, q_ref, k_ref, vt_ref, f_ref, *, q_rank, kv_rank, f_w, q_scale):
    x = x_ref[0]
    h = _rms(x, g_ref[...]).astype(BF16)
    z = _dot(h, win_ref[...])
    o = 0
    cq = z[:, o:o + q_rank]; o += q_rank
    ckv = z[:, o:o + kv_rank]; o += kv_rank
    f_ref[0] = z[:, o:o + f_w]; o += f_w
    kr = z[:, o:o + HEAD_PAD]; o += HEAD_PAD
    krr = z[:, o:o + HEAD_PAD]
    cos = cos_ref[...]
    sin = sin_ref[...]
    cqn = _rms(cq, gq_ref[...]).astype(BF16)
    q = _dot(cqn, wuq_ref[...])
    qr = _dot(cqn, wuqr_ref[...])
    ckvn = _rms(ckv, gkv_ref[...]).astype(BF16)
    k = _dot(ckvn, wuk_ref[...])
    vt = lax.dot_general(wuvt_ref[...], ckvn, (((1,), (1,)), ((), ())), preferred_element_type=F32)
    vt = vt + ones_ref[...]
    tk = vt_ref.shape[-1]
    for hd in range(MLA_HEADS):
        for c in range(vt_ref.shape[2]):
            vt_ref[0, hd, c] = vt[hd * VT_ROWS:(hd + 1) * VT_ROWS, c * tk:(c + 1) * tk].astype(BF16)
    k_rope = kr * cos + krr * sin
    for hd in range(MLA_HEADS):
        sl = slice(hd * HEAD_PAD, (hd + 1) * HEAD_PAD)
        q_ref[0, :, sl] = ((q[:, sl] * cos + qr[:, sl] * sin) * q_scale).astype(BF16)
        k_ref[0, :, sl] = (k[:, sl] + k_rope).astype(BF16)


def _mixer_in(x, g, win, gq, wuq, wuqr, gkv, wuk, wuvt, ones, cos, sin, *, q_rank, kv_rank, f_w, q_scale):
    b, s, d = x.shape
    tm = _tile(s, 512)
    tk = _tile(tm, KV_CHUNK)
    cpt = tm // tk
    hw = MLA_HEADS * HEAD_PAD
    tok = lambda w: pl.BlockSpec((1, tm, w), lambda bi, i: (bi, i, 0))
    rope = pl.BlockSpec((tm, HEAD_PAD), lambda bi, i: (i, 0))
    kern = functools.partial(_mixer_in_kernel, q_rank=q_rank, kv_rank=kv_rank, f_w=f_w, q_scale=q_scale)
    return pl.pallas_call(
        kern,
        grid=(b, s // tm),
        in_specs=[tok(d), _const_spec(g.shape), _const_spec(win.shape), _const_spec(gq.shape),
                  _const_spec(wuq.shape), _const_spec(wuqr.shape), _const_spec(gkv.shape),
                  _const_spec(wuk.shape), _const_spec(wuvt.shape), _const_spec(ones.shape), rope, rope],
        out_specs=[tok(hw), tok(hw),
                   pl.BlockSpec((1, MLA_HEADS, cpt, VT_ROWS, tk), lambda bi, i: (bi, 0, i, 0, 0)),
                   tok(f_w)],
        out_shape=[jax.ShapeDtypeStruct((b, s, hw), BF16), jax.ShapeDtypeStruct((b, s, hw), BF16),
                   jax.ShapeDtypeStruct((b, MLA_HEADS, s // tk, VT_ROWS, tk), BF16),
                   jax.ShapeDtypeStruct((b, s, f_w), F32)],
        compiler_params=_params("parallel", "parallel"),
        name="mixer_in",
    )(x, g, win, gq, wuq, wuqr, gkv, wuk, wuvt, ones, cos, sin)


def _q_transposed(q_ref, t, tq):
    return q_ref[0, t * tq:(t + 1) * tq, :].astype(F32).T.astype(BF16)


def _scores(k_ref, s_ref, qt, row0, slot):
    tk = s_ref.shape[1]
    s = _dot(k_ref[0, pl.ds(row0, tk), :], qt)
    s_ref[slot] = s
    return jnp.max(s, axis=0, keepdims=True)


def _softmax_update(vt_ref, s_ref, j, slot, m, acc, cmax):
    m_new = jnp.maximum(m, cmax)
    alpha = jnp.exp2(m - m_new)
    p = jnp.exp2(s_ref[slot] - m_new).astype(BF16)
    return m_new, alpha * acc + _dot(vt_ref[0, 0, j], p)


def _attn_init(tq):
    return jnp.full((1, tq), -jnp.inf, F32), jnp.zeros((VT_ROWS, tq), F32)


def _attn_out(acc):
    return (acc[:V_HEAD] / acc[V_HEAD:V_HEAD + 1]).astype(BF16)


def _flash_kernel(q_ref, k_ref, vt_ref, o_ref, s_ref, *, tq):
    nk, _, tk = vt_ref.shape[2:]
    nt = q_ref.shape[1] // tq
    nslots = s_ref.shape[0]
    assert nslots >= 3 and nk % nslots == 0, (nk, nslots)

    def tile(t, qt, qt_next, ca, cb):
        col0 = t * tq if isinstance(t, int) else pl.multiple_of(t * tq, tq)
        m, acc = _attn_init(tq)
        for u in range(nk):
            if u + 2 < nk:
                cn = _scores(k_ref, s_ref, qt, (u + 2) * tk, (u + 2) % nslots)
            elif qt_next is not None:
                cn = _scores(k_ref, s_ref, qt_next, (u + 2 - nk) * tk, (u + 2 - nk) % nslots)
            else:
                cn = None
            m, acc = _softmax_update(vt_ref, s_ref, u, u % nslots, m, acc, ca)
            ca, cb = cb, cn
        o_ref[0, :, pl.ds(col0, tq)] = _attn_out(acc)
        return ca, cb

    qt = _q_transposed(q_ref, 0, tq)
    c0 = _scores(k_ref, s_ref, qt, 0, 0)
    c1 = _scores(k_ref, s_ref, qt, tk, 1)

    def body(t, carry):
        qt, ca, cb = carry
        qt_next = q_ref[0, pl.ds(pl.multiple_of((t + 1) * tq, tq), tq), :].astype(F32).T.astype(BF16)
        ca, cb = tile(t, qt, qt_next, ca, cb)
        return qt_next, ca, cb

    qt, ca, cb = lax.fori_loop(0, nt - 1, body, (qt, c0, c1))
    tile(nt - 1, qt, None, ca, cb)


def _flash(q, k, vt):
    b, s, _ = q.shape
    tq = _tile(s, 1024)
    _, _, nk, _, tk = vt.shape
    return pl.pallas_call(
        functools.partial(_flash_kernel, tq=tq),
        grid=(b, MLA_HEADS),
        in_specs=[pl.BlockSpec((1, s, HEAD_PAD), lambda bi, h: (bi, 0, h)),
                  pl.BlockSpec((1, s, HEAD_PAD), lambda bi, h: (bi, 0, h)),
                  pl.BlockSpec((1, 1, nk, VT_ROWS, tk), lambda bi, h: (bi, h, 0, 0, 0))],
        out_specs=pl.BlockSpec((1, V_HEAD, s), lambda bi, h: (bi, h, 0)),
        out_shape=jax.ShapeDtypeStruct((b, MLA_HEADS * V_HEAD, s), BF16),
        scratch_shapes=[pltpu.VMEM((SCORE_SLOTS, tk, tq), F32)],
        compiler_params=_params("parallel", "parallel"),
        name="mla_flash",
    )(q, k, vt)


def _split(x):
    hi = lax.bitcast_convert_type(lax.bitcast_convert_type(x, jnp.uint32) & jnp.uint32(0xFFFF0000), F32)
    return hi.astype(BF16), (x - hi).astype(BF16)


def _split_stack(w, axis):
    hi, lo = _split(w)
    return jnp.concatenate([hi, lo], axis=axis)


def _const_times_data(wcat, x):
    m = wcat.shape[0] // 2
    x_hi, x_lo = _split(x)
    r = _dot(wcat, x_hi)
    return r[:m] + r[m:] + _dot(wcat[:m], x_lo)


def _dft_a_kernel(x_ref, w_ref, ar_ref, ai_ref):
    n1 = x_ref.shape[1]
    r = _const_times_data(w_ref[...], x_ref[0])
    ar_ref[0] = r[:n1]
    ai_ref[0] = r[n1:]


def _dft_a(x2, wcat):
    b, n1, w = x2.shape
    tc = _tile(w, 4096)
    blk = pl.BlockSpec((1, n1, tc), lambda bi, j: (bi, 0, j))
    sds = jax.ShapeDtypeStruct((b, n1, w), F32)
    return pl.pallas_call(
        _dft_a_kernel,
        grid=(b, w // tc),
        in_specs=[blk, _const_spec(wcat.shape)],
        out_specs=[blk, blk],
        out_shape=[sds, sds],
        compiler_params=_params("parallel", "parallel"),
        name="fnet_dft_a",
    )(x2, wcat)


def _dft_b_kernel(ar_ref, ai_ref, g_ref, wc_ref, o_ref, *, kb, scale):
    n2, w = ar_ref.shape[2:]
    wc = wc_ref[...]
    for j in range(kb):
        a = jnp.concatenate([ar_ref[0, j], ai_ref[0, j]], axis=1)
        p = _const_times_data(g_ref[j], a)
        yr = p[:n2, :w] - p[n2:, w:]
        yi = p[:n2, w:] + p[n2:, :w]
        lhs = jnp.concatenate(
            [jnp.concatenate([yr[:, g * FNET_CH:(g + 1) * FNET_CH], yi[:, g * FNET_CH:(g + 1) * FNET_CH]], axis=1)
             for g in range(FNET_GROUPS)], axis=0)
        l_hi, l_lo = _split(lhs)
        r = _dot(l_hi, wc)
        res = (r[:, :FNET_CH] + r[:, FNET_CH:] + _dot(l_lo, wc[:, :FNET_CH])) * scale
        for g in range(FNET_GROUPS):
            o_ref[0, :, j, g * FNET_CH:(g + 1) * FNET_CH] = res[g * n2:(g + 1) * n2]


def _dft_b(ar, ai, gcat, wc, scale):
    b, n1, n2, w = ar.shape
    kb = 8
    assert n1 % kb == 0
    a_blk = pl.BlockSpec((1, kb, n2, w), lambda bi, i: (bi, i, 0, 0))
    g_blk = pl.BlockSpec((kb,) + gcat.shape[1:], lambda bi, i: (i, 0, 0))
    return pl.pallas_call(
        functools.partial(_dft_b_kernel, kb=kb, scale=scale),
        grid=(b, n1 // kb),
        in_specs=[a_blk, a_blk, g_blk, _const_spec(wc.shape)],
        out_specs=pl.BlockSpec((1, n2, kb, w), lambda bi, i: (bi, 0, i, 0)),
        out_shape=jax.ShapeDtypeStruct((b, n2, n1, w), F32),
        compiler_params=_params("parallel", "parallel"),
        name="fnet_dft_b",
    )(ar, ai, gcat, wc)


def _dft_tables(s):
    n1 = 1 << ((s.bit_length() - 1 + 1) // 2)
    n2 = s // n1
    assert n1 * n2 == s and n1 % 8 == 0, (s, n1, n2)

    def cs(idx, period):
        ang = (2.0 * math.pi / period) * (idx % period).astype(F32)
        return jnp.cos(ang), jnp.sin(ang)

    i1 = jnp.arange(n1, dtype=jnp.int32)
    i2 = jnp.arange(n2, dtype=jnp.int32)
    c1, s1 = cs(i1[:, None] * i1[None, :], n1)
    wa = _split_stack(jnp.concatenate([c1, -s1], axis=0), 0)
    c2, s2 = cs(i2[:, None] * i2[None, :], n2)
    ct, st = cs(i1[:, None] * i2[None, :], s)
    c2, s2, ct, st = lax.optimization_barrier((c2, s2, ct, st))
    gr =c2[None] * ct[:, None, :] - s2[None] * st[:, None, :]
    gi = -(s2[None] * ct[:, None, :] + c2[None] * st[:, None, :])
    gcat = _split_stack(jnp.concatenate([gr, gi], axis=1), 1)
    ic = jnp.arange(FNET_CH, dtype=jnp.int32)
    cc, sc = cs(ic[:, None] * ic[None, :], FNET_CH)
    wc = _split_stack(jnp.concatenate([cc, sc], axis=0), 1)
    return n1, n2, wa, gcat, wc


def _fnet(f):
    b, s, w = f.shape
    n1, n2, wa, gcat, wc = _dft_tables(s)
    ar, ai = _dft_a(f.reshape(b, n1, n2 * w), wa)
    out = _dft_b(ar.reshape(b, n1, n2, w), ai.reshape(b, n1, n2, w), gcat, wc,
                 1.0 / math.sqrt(s * FNET_CH))
    return out.reshape(b, s, w)


def _mixer_out_kernel(x_ref, at_ref, f_ref, wa_ref, wf_ref, o_ref):
    attn = lax.dot_general(at_ref[0], wa_ref[...], (((0,), (0,)), ((), ())), preferred_element_type=F32)
    o_ref[0] = x_ref[0] + attn + _dot(f_ref[0].astype(BF16), wf_ref[...])


def _mixer_out(x, attn_t, fo, wa, wf):
    b, s, d = x.shape
    tm = _tile(s, 512)
    tok = lambda w: pl.BlockSpec((1, tm, w), lambda bi, i: (bi, i, 0))
    return pl.pallas_call(
        _mixer_out_kernel,
        grid=(b, s // tm),
        in_specs=[tok(d), pl.BlockSpec((1, attn_t.shape[1], tm), lambda bi, i: (bi, 0, i)),
                  tok(fo.shape[-1]), _const_spec(wa.shape), _const_spec(wf.shape)],
        out_specs=tok(d),
        out_shape=jax.ShapeDtypeStruct(x.shape, F32),
        compiler_params=_params("parallel", "parallel"),
        name="mixer_out",
    )(x, attn_t, fo, wa, wf)


def _conv_in_kernel(x_ref, g_ref, w_ref, b_ref, u_ref):
    d = x_ref.shape[-1]
    h = _rms(x_ref[0], g_ref[...]).astype(BF16)
    z = _dot(h, w_ref[...]) + b_ref[...]
    u_ref[0] = z[:, :d] * jax.nn.sigmoid(z[:, d:])


def _conv_in(x, g, w, bias):
    b, s, d = x.shape
    tm = _tile(s, 512)
    tok = pl.BlockSpec((1, tm, d), lambda bi, i: (bi, i, 0))
    return pl.pallas_call(
        _conv_in_kernel,
        grid=(b, s // tm),
        in_specs=[tok, _const_spec(g.shape), _const_spec(w.shape), _const_spec(bias.shape)],
        out_specs=tok,
        out_shape=jax.ShapeDtypeStruct(x.shape, F32),
        compiler_params=_params("parallel", "parallel"),
        name="conv_in",
    )(x, g, w, bias)


HALO = 16


def _conv_out_kernel(x_ref, u_ref, ul_ref, ur_ref, wdw_ref, bdw_ref, gln_ref, bln_ref, w2_ref, b2_ref,
                     o_ref, buf_ref, c_ref, *, taps, rows):
    i = pl.program_id(1)
    n = pl.num_programs(1)
    tm = u_ref.shape[1]
    d = u_ref.shape[2]
    half = taps // 2
    nlb = d // LANES
    for cb in range(nlb):
        ls = slice(cb * LANES, (cb + 1) * LANES)
        buf_ref[cb, 0:HALO] = jnp.where(i > 0, ul_ref[0, :, ls], 0.0)
        buf_ref[cb, HALO:HALO + tm] = u_ref[0, :, ls]
        buf_ref[cb, HALO + tm:HALO + tm + HALO] = jnp.where(i < n - 1, ur_ref[0, :, ls], 0.0)

    def lane_block(cb, _):
        for r0 in range(0, tm, rows):
            acc = jnp.zeros((rows, LANES), F32)
            for t in range(taps):
                lo = r0 + HALO - half + t
                acc = acc + buf_ref[cb, lo:lo + rows] * wdw_ref[cb, t:t + 1]
            c_ref[cb, r0:r0 + rows] = acc
        return 0

    lax.fori_loop(0, nlb, lane_block, 0)
    c = jnp.concatenate([c_ref[cb] for cb in range(nlb)], axis=1) + bdw_ref[...]
    mu = jnp.mean(c, axis=-1, keepdims=True)
    cc = c - mu
    y = cc * lax.rsqrt(jnp.mean(cc * cc, axis=-1, keepdims=True) + EPS)
    y = y * gln_ref[...] + bln_ref[...]
    y = y * jax.nn.sigmoid(y)
    o_ref[0] = x_ref[0] + _dot(y.astype(BF16), w2_ref[...]) + b2_ref[...]


def _conv_out(x, u, wdw, bdw, gln, bln, w2, b2):
    b, s, d = x.shape
    taps = wdw.shape[0]
    assert taps // 2 <= HALO
    tm = _tile(s, 256)
    hb = tm // HALO
    nh = s // HALO
    tok = pl.BlockSpec((1, tm, d), lambda bi, i: (bi, i, 0))
    left = pl.BlockSpec((1, HALO, d), lambda bi, i: (bi, jnp.maximum(i * hb - 1, 0), 0))
    right = pl.BlockSpec((1, HALO, d), lambda bi, i: (bi, jnp.minimum((i + 1) * hb, nh - 1), 0))
    kern = functools.partial(_conv_out_kernel, taps=taps, rows=32)
    nlb = d // LANES
    wdw = wdw.reshape(taps, nlb, LANES).transpose(1, 0, 2)
    return pl.pallas_call(
        kern,
        grid=(b, s // tm),
        in_specs=[tok, tok, left, right, _const_spec(wdw.shape), _const_spec(bdw.shape),
                  _const_spec(gln.shape), _const_spec(bln.shape), _const_spec(w2.shape), _const_spec(b2.shape)],
        out_specs=tok,
        out_shape=jax.ShapeDtypeStruct(x.shape, F32),
        scratch_shapes=[pltpu.VMEM((nlb, tm + 2 * HALO, LANES), F32), pltpu.VMEM((nlb, tm, LANES), F32)],
        compiler_params=_params("parallel", "parallel"),
        name="conv_out",
    )(x, u, u, u, wdw, bdw, gln, bln, w2, b2)


def _mem_kv_kernel(m_ref, g_ref, w_ref, o_ref):
    h = _rms(m_ref[0], g_ref[...]).astype(BF16)
    o_ref[0] = _dot(h, w_ref[...]).astype(BF16)


def _mem_kv(mem, g, wkv):
    b, nm, d = mem.shape
    n = wkv.shape[1]
    return pl.pallas_call(
        _mem_kv_kernel,
        grid=(b,),
        in_specs=[pl.BlockSpec((1, nm, d), lambda bi: (bi, 0, 0)), _const_spec(g.shape), _const_spec(wkv.shape)],
        out_specs=pl.BlockSpec((1, nm, n), lambda bi: (bi, 0, 0)),
        out_shape=jax.ShapeDtypeStruct((b, nm, n), BF16),
        compiler_params=_params("parallel"),
        name="mem_kv",
    )(mem, g, wkv)


def _xattn_block(x, g_ref, wq_ref, kv_ref, wo_ref, scale):
    d = x.shape[-1]
    hd = d // XA_HEADS
    h = _rms(x, g_ref[...]).astype(BF16)
    q = (_dot(h, wq_ref[...]) * scale).astype(BF16)
    outs = []
    for a in range(XA_HEADS):
        k = kv_ref[0, :, a * hd:(a + 1) * hd]
        v = kv_ref[0, :, d + a * hd:d + (a + 1) * hd]
        s = lax.dot_general(q[:, a * hd:(a + 1) * hd], k, (((1,), (1,)), ((), ())),
                            preferred_element_type=F32)
        p = jnp.exp(s - jnp.max(s, axis=1, keepdims=True))
        l = jnp.sum(p, axis=1, keepdims=True)
        outs.append((_dot(p.astype(BF16), v) / l).astype(BF16))
    o = jnp.concatenate(outs, axis=1)
    return x + _dot(o, wo_ref[...])


def _ffn_block(x, g_ref, w1_ref, w3_ref, w2_ref):
    h = _rms(x, g_ref[...]).astype(BF16)
    acc = x
    for c0 in range(0, w1_ref.shape[1], FF_CHUNK):
        a = _dot(h, w1_ref[:, c0:c0 + FF_CHUNK])
        bb = _dot(h, w3_ref[:, c0:c0 + FF_CHUNK])
        acc = acc + _dot((a * jax.nn.sigmoid(a) * bb).astype(BF16), w2_ref[c0:c0 + FF_CHUNK, :])
    return acc


def _xattn_ffn_kernel(x_ref, gx_ref, wq_ref, kv_ref, wo_ref, gf_ref, w1_ref, w3_ref, w2_ref, gn_ref, o_ref,
                      *, scale, final_norm):
    x = _xattn_block(x_ref[0], gx_ref, wq_ref, kv_ref, wo_ref, scale)
    x = _ffn_block(x, gf_ref, w1_ref, w3_ref, w2_ref)
    if final_norm:
        x = _rms(x, gn_ref[...])
    o_ref[0] = x


def _xattn_ffn(x, gx, wq, kv, wo, gf, w1, w3, w2, gn, final_norm):
    b, s, d = x.shape
    tm = _tile(s, 512)
    tok = pl.BlockSpec((1, tm, d), lambda bi, i: (bi, i, 0))
    assert w1.shape[1] % FF_CHUNK == 0
    wspec = lambda w: pl.BlockSpec(w.shape, lambda bi, i: (0, 0), pipeline_mode=pl.Buffered(1))
    return pl.pallas_call(
        functools.partial(_xattn_ffn_kernel, scale=(d // XA_HEADS) ** -0.5, final_norm=final_norm),
        grid=(b, s // tm),
        in_specs=[tok, _const_spec(gx.shape), wspec(wq),
                  pl.BlockSpec((1,) + kv.shape[1:], lambda bi, i: (bi, 0, 0)), wspec(wo),
                  _const_spec(gf.shape), wspec(w1), wspec(w3), wspec(w2), _const_spec(gn.shape)],
        out_specs=tok,
        out_shape=jax.ShapeDtypeStruct(x.shape, F32),
        compiler_params=_params("parallel", "parallel"),
        name="xattn_ffn",
    )(x, gx, wq, kv, wo, gf, w1, w3, w2, gn)


def _rope_tables(s):
    half = QK_ROPE // 2
    inv = ROPE_THETA ** (-jnp.arange(half, dtype=F32) / half)
    ang = jnp.arange(s, dtype=F32)[:, None] * inv[None, :]
    cos = jnp.cos(ang)
    sin = jnp.sin(ang)
    ones = jnp.ones((s, QK_NOPE), F32)
    tail = HEAD_PAD - QK_NOPE - QK_ROPE
    cos_t = jnp.concatenate([ones, cos, cos, jnp.ones((s, tail), F32)], axis=1)
    sin_t = jnp.concatenate([0 * ones, -sin, sin, jnp.zeros((s, tail), F32)], axis=1)
    return cos_t, sin_t


def _swap_halves(w):
    half = QK_ROPE // 2
    return jnp.concatenate([w[..., half:], w[..., :half]], axis=-1)


def _prep_mixer(w_in, w_uq, w_ukv, w_out):
    d = w_in.shape[0]
    q_rank = w_uq.shape[0]
    kv_rank = w_ukv.shape[0]
    f_w = FNET_GROUPS * FNET_CH
    tail = HEAD_PAD - QK_NOPE - QK_ROPE
    o = q_rank + kv_rank
    w_kr = w_in[:, o:o + QK_ROPE]
    pad_rope = lambda w: jnp.concatenate(
        [jnp.zeros(w.shape[:-1] + (QK_NOPE,), F32), w, jnp.zeros(w.shape[:-1] + (tail,), F32)], axis=-1)
    win = jnp.concatenate([w_in[:, :o], w_in[:, o + QK_ROPE:], pad_rope(w_kr), pad_rope(_swap_halves(w_kr))],
                          axis=1).astype(BF16)
    uq = w_uq.reshape(q_rank, MLA_HEADS, QK_NOPE + QK_ROPE)
    zt = jnp.zeros((q_rank, MLA_HEADS, tail), F32)
    wuq = jnp.concatenate([uq, zt], axis=-1).reshape(q_rank, -1).astype(BF16)
    wuqr = jnp.concatenate([jnp.zeros((q_rank, MLA_HEADS, QK_NOPE), F32), _swap_halves(uq[..., QK_NOPE:]), zt],
                           axis=-1).reshape(q_rank, -1).astype(BF16)
    ukv = w_ukv.reshape(kv_rank, MLA_HEADS, QK_NOPE + V_HEAD)
    wuk = jnp.concatenate([ukv[..., :QK_NOPE], jnp.zeros((kv_rank, MLA_HEADS, HEAD_PAD - QK_NOPE), F32)],
                          axis=-1).reshape(kv_rank, -1).astype(BF16)
    wuvt = jnp.concatenate([ukv[..., QK_NOPE:], jnp.zeros((kv_rank, MLA_HEADS, VT_ROWS - V_HEAD), F32)],
                           axis=-1).reshape(kv_rank, -1).T.astype(BF16)
    ones = (jnp.arange(MLA_HEADS * VT_ROWS) % VT_ROWS == V_HEAD).astype(F32).reshape(-1, 1)
    av = MLA_HEADS * V_HEAD
    return dict(win=win, wuq=wuq, wuqr=wuqr, wuk=wuk, wuvt=wuvt, ones=ones,
                wa=w_out[:av].astype(BF16), wf=w_out[av:].astype(BF16),
                q_rank=q_rank, kv_rank=kv_rank, f_w=f_w)


def _row(v):
    return v.reshape(1, -1).astype(F32)


def _trunk(x, mem, p):
    depth = p['g_mix'].shape[0]
    s = x.shape[1]
    for i in range(depth):
        j = i // 2
        if i % 2 == 0:
            mp = p['mixer'][j]
            cos_t, sin_t = _rope_tables(s)
            q_scale = (QK_NOPE + QK_ROPE) ** -0.5 * math.log2(math.e)
            q, k, vt, f = _mixer_in(x, _row(p['g_mix'][i]), mp['win'], _row(p['mla_g_q'][j]), mp['wuq'], mp['wuqr'],
                                    _row(p['mla_g_kv'][j]), mp['wuk'], mp['wuvt'], mp['ones'], cos_t, sin_t,
                                    q_rank=mp['q_rank'], kv_rank=mp['kv_rank'], f_w=mp['f_w'], q_scale=q_scale)
            attn_t = _flash(q, k, vt)
            fo = _fnet(f)
            x = _mixer_out(x, attn_t, fo, mp['wa'], mp['wf'])
        else:
            u = _conv_in(x, _row(p['g_mix'][i]), p['conv_w_pw1'][j], _row(p['conv_b_pw1'][j]))
            x = _conv_out(x, u, p['conv_w_dw'][j], _row(p['conv_b_dw'][j]), _row(p['conv_g_ln'][j]),
                          _row(p['conv_b_ln'][j]), p['conv_w_pw2'][j], _row(p['conv_b_pw2'][j]))
        kv = _mem_kv(mem, _row(p['g_xkv'][i]), p['xa_wkv'][i])
        x = _xattn_ffn(x, _row(p['g_xq'][i]), p['xa_wq'][i], kv, p['xa_wo'][i],
                       _row(p['g_ffn'][i]), p['ffn_w1'][i], p['ffn_w3'][i], p['ffn_w2'][i],
                       _row(p['g_final']), final_norm=(i == depth - 1))
    return x


def kernel(x_prompt, x_sample, mem_prompt, mem_sample, g_mix, g_xq, g_xkv, xa_wq, xa_wk, xa_wv, xa_wo, g_ffn, ffn_w1, ffn_w3, ffn_w2, ab_w_in, mla_g_q, mla_w_uq, mla_g_kv, mla_w_ukv, ab_w_out, conv_w_pw1, conv_b_pw1, conv_w_dw, conv_b_dw, conv_g_ln, conv_b_ln, conv_w_pw2, conv_b_pw2, g_final):
    p = {
        'g_mix': g_mix, 'g_xq': g_xq, 'g_xkv': g_xkv, 'g_ffn': g_ffn, 'g_final': g_final,
        'xa_wq': xa_wq.astype(BF16), 'xa_wo': xa_wo.astype(BF16),
        'xa_wkv': jnp.concatenate([xa_wk, xa_wv], axis=-1).astype(BF16),
        'ffn_w1': ffn_w1.astype(BF16), 'ffn_w3': ffn_w3.astype(BF16), 'ffn_w2': ffn_w2.astype(BF16),
        'mla_g_q': mla_g_q, 'mla_g_kv': mla_g_kv,
        'mixer': [_prep_mixer(ab_w_in[j], mla_w_uq[j], mla_w_ukv[j], ab_w_out[j])
                  for j in range(ab_w_in.shape[0])],
        'conv_w_pw1': conv_w_pw1.astype(BF16), 'conv_b_pw1': conv_b_pw1,
        'conv_w_dw': conv_w_dw, 'conv_b_dw': conv_b_dw, 'conv_g_ln': conv_g_ln, 'conv_b_ln': conv_b_ln,
        'conv_w_pw2': conv_w_pw2.astype(BF16), 'conv_b_pw2': conv_b_pw2,
    }
    return (_trunk(x_prompt, mem_prompt, p), _trunk(x_sample, mem_sample, p))
```
